```python
import math
import jax
import jax.numpy as jnp
from jax import lax
import numpy as np

D_MODEL = 1024
BATCH = 16
SEQ = 2048
DEPTH = 2

CTX_LEN = 256
GRID_W = 64
ROPE_THETA = 10000.0
LN_EPS = 1e-6
RMS_EPS = 1e-5
NEG_INF = -1e30
Q_BLOCK = 128

A_GROUPS = 4
A_GROUP_DIM = 128
A_WIDTH = A_GROUPS * A_GROUP_DIM
CHUNK = 128

B_HEADS = 4
B_HEAD_DIM = 64
B_V_DIM = 2 * B_HEAD_DIM
B_QK_WIDTH = B_HEADS * 2 * B_HEAD_DIM
B_WIDTH = B_HEADS * B_V_DIM

C_WIDTH = 512
C_KERNEL = 31

D_HEADS = 8
D_KV_HEADS = 2
D_GROUP = D_HEADS // D_KV_HEADS
D_HEAD_DIM = 64
D_WIDTH = D_HEADS * D_HEAD_DIM
D_KV_WIDTH = D_KV_HEADS * D_HEAD_DIM
WINDOW = 128

AB_CUTS = (A_WIDTH, 2 * A_WIDTH, 3 * A_WIDTH, 3 * A_WIDTH + B_QK_WIDTH,
           3 * A_WIDTH + 2 * B_QK_WIDTH, 3 * A_WIDTH + 2 * B_QK_WIDTH + B_WIDTH)
AB_IN = AB_CUTS[-1] + B_WIDTH
AB_MIX = A_WIDTH + B_WIDTH
CD_CUTS = (C_WIDTH, 2 * C_WIDTH, 3 * C_WIDTH, 3 * C_WIDTH + D_WIDTH,
           3 * C_WIDTH + D_WIDTH + D_KV_WIDTH, 3 * C_WIDTH + D_WIDTH + 2 * D_KV_WIDTH)
CD_IN = CD_CUTS[-1] + D_WIDTH
CD_MIX = C_WIDTH + D_WIDTH

N_AB = (DEPTH + 1) // 2
N_CD = DEPTH // 2

kernel_name = "hybrid_diffusion_gmlp_diffattn_conformer_swa"


def layer_norm(x, g, b):
    xf = x.astype(jnp.float32)
    mu = jnp.mean(xf, axis=-1, keepdims=True)
    var = jnp.mean(jnp.square(xf - mu), axis=-1, keepdims=True)
    return ((xf - mu) * lax.rsqrt(var + LN_EPS)).astype(x.dtype) * g + b


def rms_norm(x, g):
    xf = x.astype(jnp.float32)
    return (xf * lax.rsqrt(jnp.mean(xf * xf, axis=-1, keepdims=True) + RMS_EPS)).astype(x.dtype) * g


def axial_rope_tables(rows, head_dim):
    m = head_dim // 4
    inv = ROPE_THETA ** (-jnp.arange(m, dtype=jnp.float32) / m)
    r, col = jnp.meshgrid(jnp.arange(rows, dtype=jnp.float32), jnp.arange(GRID_W, dtype=jnp.float32), indexing="ij")
    ang_r = r.reshape(-1, 1) * inv
    ang_c = col.reshape(-1, 1) * inv
    return (jnp.cos(ang_r), jnp.sin(ang_r), jnp.cos(ang_c), jnp.sin(ang_c))


def _rotate(x, cos, sin):
    x1, x2 = jnp.split(x, 2, axis=-1)
    cos = cos[None, :, None, :].astype(x.dtype)
    sin = sin[None, :, None, :].astype(x.dtype)
    return jnp.concatenate([x1 * cos - x2 * sin, x2 * cos + x1 * sin], axis=-1)


def apply_axial_rope(x, rope):
    cr, sr, cc, sc = rope
    xr, xc = jnp.split(x, 2, axis=-1)
    return jnp.concatenate([_rotate(xr, cr, sr), _rotate(xc, cc, sc)], axis=-1)


def chunk_gmlp(u, v, w_s, b_s, g, bb):
    bsz, n, _ = v.shape
    u = jax.nn.gelu(u)
    v = layer_norm(jax.nn.gelu(v), g, bb)
    vc = v.reshape(bsz, n // CHUNK, CHUNK, A_GROUPS, A_GROUP_DIM)
    mixed = jnp.einsum("gpq,bcqgd->bcpgd", w_s, vc) + b_s.T[None, None, :, :, None]
    return u * mixed.reshape(bsz, n, A_WIDTH)


def diff_attend(q, k, v, lam):
    s = jnp.einsum("bqhcd,bkhcd->bhcqk", q, k).astype(jnp.float32) * (B_HEAD_DIM ** -0.5)
    p = jax.nn.softmax(s, axis=-1)
    w = p[:, :, 0] - lam * p[:, :, 1]
    return jnp.einsum("bhqk,bkhd->bqhd", w.astype(v.dtype), v)


def diff_heads_out(o, subln_g, lam_init):
    bsz, n = o.shape[0], o.shape[1]
    return (rms_norm(o, subln_g) * (1.0 - lam_init)).reshape(bsz, n, B_WIDTH)


def conformer_conv(a, b, dw_w, dw_b, g, bb):
    h = a * jax.nn.sigmoid(b)
    y = lax.conv_general_dilated(h, dw_w[:, None, :].astype(h.dtype), window_strides=(1,),
                                 padding=[(C_KERNEL // 2, C_KERNEL // 2)],
                                 dimension_numbers=("NWC", "WIO", "NWC"),
                                 feature_group_count=C_WIDTH) + dw_b
    return jax.nn.silu(layer_norm(y, g, bb))


def sink_gqa_attend(q, k, v, mask, sink):
    s = jnp.einsum("bqhgd,bkhd->bhgqk", q, k).astype(jnp.float32) * (D_HEAD_DIM ** -0.5)
    if mask is not None:
        s = jnp.where(mask, s, NEG_INF)
    sink_col = jnp.broadcast_to(sink.astype(jnp.float32)[None, :, :, None, None], s.shape[:-1] + (1,))
    p = jax.nn.softmax(jnp.concatenate([s, sink_col], axis=-1), axis=-1)[..., :-1]
    o = jnp.einsum("bhgqk,bkhd->bqhgd", p.astype(v.dtype), v)
    return o.reshape(o.shape[0], o.shape[1], D_WIDTH)


def ab_sublayer(u_lat, u_ctx, need_ctx, rope, layer, w_in, w_out, w_s, b_s, an_g, an_b,
                lq1, lk1, lq2, lk2, subln_g):
    bsz, n, _ = u_lat.shape
    au, av, ag, bq, bk, bv, bg = jnp.split(u_lat @ w_in, AB_CUTS, axis=-1)
    if need_ctx:
        cau, cav, cag, cbq, cbk, cbv, cbg = jnp.split(u_ctx @ w_in, AB_CUTS, axis=-1)
    else:
        cbk, cbv = jnp.split(u_ctx @ w_in[:, AB_CUTS[3]:AB_CUTS[5]], [B_QK_WIDTH], axis=-1)
    lam_init = 0.8 - 0.6 * math.exp(-0.3 * layer)
    lam = (jnp.exp(jnp.sum(lq1.astype(jnp.float32) * lk1.astype(jnp.float32)))
           - jnp.exp(jnp.sum(lq2.astype(jnp.float32) * lk2.astype(jnp.float32))) + lam_init)
    q = apply_axial_rope(bq.reshape(bsz, n, 2 * B_HEADS, B_HEAD_DIM), rope).reshape(bsz, n, B_HEADS, 2, B_HEAD_DIM)
    k = apply_axial_rope(bk.reshape(bsz, n, 2 * B_HEADS, B_HEAD_DIM), rope).reshape(bsz, n, B_HEADS, 2, B_HEAD_DIM)
    v = bv.reshape(bsz, n, B_HEADS, B_V_DIM)
    ck = cbk.reshape(bsz, CTX_LEN, B_HEADS, 2, B_HEAD_DIM)
    cv = cbv.reshape(bsz, CTX_LEN, B_HEADS, B_V_DIM)
    k_all = jnp.concatenate([ck, k], axis=1)
    v_all = jnp.concatenate([cv, v], axis=1)
    q_blocks = q.reshape(bsz, n // Q_BLOCK, Q_BLOCK, B_HEADS, 2, B_HEAD_DIM).swapaxes(0, 1)
    o = lax.map(lambda qb: diff_attend(qb, k_all, v_all, lam), q_blocks)
    b_lat = diff_heads_out(o.swapaxes(0, 1).reshape(bsz, n, B_HEADS, B_V_DIM), subln_g, lam_init)
    a_lat = chunk_gmlp(au, av, w_s, b_s, an_g, an_b)
    y_lat = jnp.concatenate([a_lat * jax.nn.silu(ag), b_lat * jax.nn.silu(bg)], axis=-1) @ w_out
    y_ctx = None
    if need_ctx:
        cq = cbq.reshape(bsz, CTX_LEN, B_HEADS, 2, B_HEAD_DIM)
        b_ctx = diff_heads_out(diff_attend(cq, ck, cv, lam), subln_g, lam_init)
        a_ctx = chunk_gmlp(cau, cav, w_s, b_s, an_g, an_b)
        y_ctx = jnp.concatenate([a_ctx * jax.nn.silu(cag), b_ctx * jax.nn.silu(cbg)], axis=-1) @ w_out
    return y_lat, y_ctx


def cd_sublayer(u_lat, u_ctx, need_ctx, rope, w_in, w_out, dw_w, dw_b, cn_g, cn_b, sink_logit):
    bsz, n, _ = u_lat.shape
    ca, cb, cg, dq, dk, dv, dg = jnp.split(u_lat @ w_in, CD_CUTS, axis=-1)
    if need_ctx:
        cca, ccb, ccg, cdq, cdk, cdv, cdg = jnp.split(u_ctx @ w_in, CD_CUTS, axis=-1)
    else:
        cdk, cdv = jnp.split(u_ctx @ w_in[:, CD_CUTS[3]:CD_CUTS[5]], [D_KV_WIDTH], axis=-1)
    ck = cdk.reshape(bsz, CTX_LEN, D_KV_HEADS, D_HEAD_DIM)
    cv = cdv.reshape(bsz, CTX_LEN, D_KV_HEADS, D_HEAD_DIM)
    sink = sink_logit.reshape(D_KV_HEADS, D_GROUP)
    q = apply_axial_rope(dq.reshape(bsz, n, D_HEADS, D_HEAD_DIM), rope).reshape(bsz, n, D_KV_HEADS, D_GROUP, D_HEAD_DIM)
    k = apply_axial_rope(dk.reshape(bsz, n, D_KV_HEADS, D_HEAD_DIM), rope)
    v = dv.reshape(bsz, n, D_KV_HEADS, D_HEAD_DIM)
    pad = ((0, 0), (WINDOW, WINDOW), (0, 0), (0, 0))
    kp, vp = jnp.pad(k, pad), jnp.pad(v, pad)
    band_len = Q_BLOCK + 2 * WINDOW
    ctx_mask = jnp.ones((Q_BLOCK, CTX_LEN), dtype=bool)

    def band_block(i):
        start = i * Q_BLOCK
        qb = lax.dynamic_slice_in_dim(q, start, Q_BLOCK, axis=1)
        kb = lax.dynamic_slice_in_dim(kp, start, band_len, axis=1)
        vb = lax.dynamic_slice_in_dim(vp, start, band_len, axis=1)
        qpos = start + jnp.arange(Q_BLOCK)
        kpos = start - WINDOW + jnp.arange(band_len)
        band = ((jnp.abs(qpos[:, None] - kpos[None, :]) <= WINDOW)
                & (kpos >= 0)[None, :] & (kpos < n)[None, :])
        mask = jnp.concatenate([band, ctx_mask], axis=1)
        return sink_gqa_attend(qb, jnp.concatenate([kb, ck], axis=1), jnp.concatenate([vb, cv], axis=1), mask, sink)

    d_lat = lax.map(band_block, jnp.arange(n // Q_BLOCK)).swapaxes(0, 1).reshape(bsz, n, D_WIDTH)
    c_lat = conformer_conv(ca, cb, dw_w, dw_b, cn_g, cn_b)
    y_lat = jnp.concatenate([c_lat * jax.nn.silu(cg), d_lat * jax.nn.silu(dg)], axis=-1) @ w_out
    y_ctx = None
    if need_ctx:
        cq = cdq.reshape(bsz, CTX_LEN, D_KV_HEADS, D_GROUP, D_HEAD_DIM)
        d_ctx = sink_gqa_attend(cq, ck, cv, None, sink)
        c_ctx_out = conformer_conv(cca, ccb, dw_w, dw_b, cn_g, cn_b)
        y_ctx = jnp.concatenate([c_ctx_out * jax.nn.silu(ccg), d_ctx * jax.nn.silu(cdg)], axis=-1) @ w_out
    return y_lat, y_ctx


def setup_inputs(seed: int = 0) -> dict:
    key = jax.random.key(seed)
    ks = jax.random.split(key, 32)
    f32 = jnp.float32
    beta = (8.0 * DEPTH) ** -0.25

    def nrm(k, shape, s):
        return jax.random.normal(k, shape, f32) * s

    return {
        "x": nrm(ks[0], (BATCH, SEQ, D_MODEL), 1.0),
        "c": nrm(ks[1], (BATCH, D_MODEL), 1.0),
        "ctx": nrm(ks[2], (BATCH, CTX_LEN, D_MODEL), 1.0),
        "c_ctx": nrm(ks[3], (D_MODEL,), 1.0),
        "mod_w": nrm(ks[4], (DEPTH, D_MODEL, 3 * D_MODEL), D_MODEL ** -0.5),
        "mod_b": nrm(ks[5], (DEPTH, 3 * D_MODEL), 0.02),
        "ln_g": 1.0 + nrm(ks[6], (DEPTH, D_MODEL), 0.02),
        "ln_b": nrm(ks[7], (DEPTH, D_MODEL), 0.02),
        "ab_w_in": nrm(ks[8], (N_AB, D_MODEL, AB_IN), D_MODEL ** -0.5),
        "ab_w_out": nrm(ks[9], (N_AB, AB_MIX, D_MODEL), beta * AB_MIX ** -0.5),
        "a_w_s": nrm(ks[10], (N_AB, A_GROUPS, CHUNK, CHUNK), CHUNK ** -0.5),
        "a_b_s": 1.0 + nrm(ks[11], (N_AB, A_GROUPS, CHUNK), 0.02),
        "a_norm_g": 1.0 + nrm(ks[12], (N_AB, A_WIDTH), 0.02),
        "a_norm_b": nrm(ks[13], (N_AB, A_WIDTH), 0.02),
        "b_lq1": nrm(ks[14], (N_AB, B_HEAD_DIM), 0.1),
        "b_lk1": nrm(ks[15], (N_AB, B_HEAD_DIM), 0.1),
        "b_lq2": nrm(ks[16], (N_AB, B_HEAD_DIM), 0.1),
        "b_lk2": nrm(ks[17], (N_AB, B_HEAD_DIM), 0.1),
        "b_subln_g": 1.0 + nrm(ks[18], (N_AB, B_V_DIM), 0.02),
        "cd_w_in": nrm(ks[19], (N_CD, D_MODEL, CD_IN), D_MODEL ** -0.5),
        "cd_w_out": nrm(ks[20], (N_CD, CD_MIX, D_MODEL), beta * CD_MIX ** -0.5),
        "c_dw_w": nrm(ks[21], (N_CD, C_KERNEL, C_WIDTH), C_KERNEL ** -0.5),
        "c_dw_b": nrm(ks[22], (N_CD, C_WIDTH), 0.02),
        "c_norm_g": 1.0 + nrm(ks[23], (N_CD, C_WIDTH), 0.02),
        "c_norm_b": nrm(ks[24], (N_CD, C_WIDTH), 0.02),
        "d_sink": nrm(ks[25], (N_CD, D_HEADS), 0.5),
    }


def reference(x, c, ctx, c_ctx, mod_w, mod_b, ln_g, ln_b, ab_w_in, ab_w_out, a_w_s, a_b_s,
              a_norm_g, a_norm_b, b_lq1, b_lk1, b_lq2, b_lk2, b_subln_g, cd_w_in, cd_w_out,
              c_dw_w, c_dw_b, c_norm_g, c_norm_b, d_sink):
    n = x.shape[1]
    rows = n // GRID_W
    rope = axial_rope_tables(rows, B_HEAD_DIM)
    alpha = (2.0 * DEPTH) ** 0.25
    silu_c = jax.nn.silu(c)
    silu_cc = jax.nn.silu(c_ctx)
    h_lat, h_ctx = x, ctx
    for layer in range(DEPTH):
        need_ctx = layer < DEPTH - 1
        shift, scale, gate = jnp.split(silu_c @ mod_w[layer] + mod_b[layer], 3, axis=-1)
        shift_c, scale_c, gate_c = jnp.split(silu_cc @ mod_w[layer] + mod_b[layer], 3, axis=-1)
        u_lat = h_lat * (1.0 + scale[:, None, :]) + shift[:, None, :]
        u_ctx = h_ctx * (1.0 + scale_c) + shift_c
        if layer % 2 == 0:
            i = layer // 2
            y_lat, y_ctx = ab_sublayer(u_lat, u_ctx, need_ctx, rope, layer, ab_w_in[i], ab_w_out[i],
                                       a_w_s[i], a_b_s[i], a_norm_g[i], a_norm_b[i],
                                       b_lq1[i], b_lk1[i], b_lq2[i], b_lk2[i], b_subln_g[i])
        else:
            i = layer // 2
            y_lat, y_ctx = cd_sublayer(u_lat, u_ctx, need_ctx, rope, cd_w_in[i], cd_w_out[i],
                                       c_dw_w[i], c_dw_b[i], c_norm_g[i], c_norm_b[i], d_sink[i])
        new_lat = layer_norm(alpha * h_lat + gate[:, None, :] * y_lat, ln_g[layer], ln_b[layer])
        if need_ctx:
            h_ctx = layer_norm(alpha * h_ctx + gate_c * y_ctx, ln_g[layer], ln_b[layer])
        h_lat = new_lat
    return h_lat
```

```python
import functools
import math

import jax
import jax.numpy as jnp
from jax import lax
from jax.experimental import pallas as pl
from jax.experimental.pallas import tpu as pltpu

F32 = jnp.float32
BF16 = jnp.bfloat16

D_MODEL = 1024
DEPTH = 2
CTX_LEN = 256
GRID_W = 64
ROPE_THETA = 10000.0
LN_EPS = 1e-6
RMS_EPS = 1e-5
NEG_INF = -1e30
HEAD_DIM = 64
BRANCH = 512
CHUNK = 128
C_KERNEL = 31
CONV_HALO = 16
MOD_ROWS = 24
LANES = 128

VMEM_LIMIT = 56 * 1024 * 1024

_NT = (((1,), (1,)), ((), ()))


def _params(n_axes, vmem=VMEM_LIMIT):
    return pltpu.CompilerParams(dimension_semantics=("arbitrary",) * n_axes, vmem_limit_bytes=vmem)


def _layer_norm(x, g, b):
    mu = jnp.mean(x, axis=-1, keepdims=True)
    xc = x - mu
    var = jnp.mean(xc * xc, axis=-1, keepdims=True)
    return xc * lax.rsqrt(var + LN_EPS) * g + b


def _lane_mask(shape, lo):
    lane = lax.broadcasted_iota(jnp.int32, shape, len(shape) - 1) % LANES
    return (lane < HEAD_DIM) if lo else (lane >= HEAD_DIM)


def _mod_kernel(c_ref, w_ref, b_ref, o_ref):
    c = c_ref[...]
    a = c * jax.nn.sigmoid(c)
    a_hi = a.astype(BF16)
    a_lo = (a - a_hi.astype(F32)).astype(BF16)
    w = w_ref[0]
    w_hi = w.astype(BF16)
    w_lo = (w - w_hi.astype(F32)).astype(BF16)
    acc = jnp.dot(a_hi, w_hi, preferred_element_type=F32)
    acc = acc + jnp.dot(a_hi, w_lo, preferred_element_type=F32)
    acc = acc + jnp.dot(a_lo, w_hi, preferred_element_type=F32)
    o_ref[0] = acc + b_ref[0]


def _modulation(cc, mod_w, mod_b):
    tn = 768
    n_out = 3 * D_MODEL
    return pl.pallas_call(
        _mod_kernel,
        grid=(DEPTH, n_out // tn),
        in_specs=[
            pl.BlockSpec((MOD_ROWS, D_MODEL), lambda l, j: (0, 0)),
            pl.BlockSpec((1, D_MODEL, tn), lambda l, j: (l, 0, j)),
            pl.BlockSpec((1, 1, tn), lambda l, j: (l, 0, j)),
        ],
        out_specs=pl.BlockSpec((1, MOD_ROWS, tn), lambda l, j: (l, 0, j)),
        out_shape=jax.ShapeDtypeStruct((DEPTH, MOD_ROWS, n_out), F32),
        compiler_params=_params(2),
        name="modulation",
    )(cc, mod_w, mod_b.reshape(DEPTH, 1, n_out))


def _mod_specs(row_fn, parts):
    return [pl.BlockSpec((1, 1, D_MODEL), functools.partial(lambda b, i, p: (row_fn(b), 0, p), p=p))
            for p in parts]


def _rope_tables(n):
    m = HEAD_DIM // 4
    inv = ROPE_THETA ** (-jnp.arange(m, dtype=F32) / m)
    t = jnp.arange(n, dtype=jnp.int32)
    ang_r = (t // GRID_W).astype(F32)[:, None] * inv
    ang_c = (t % GRID_W).astype(F32)[:, None] * inv
    z = jnp.zeros_like(ang_r)
    cos = jnp.concatenate([jnp.cos(ang_r)] * 2 + [jnp.cos(ang_c)] * 2, axis=-1)
    sin_up = jnp.concatenate([-jnp.sin(ang_r), z, -jnp.sin(ang_c), z], axis=-1)
    sin_dn = jnp.concatenate([z, jnp.sin(ang_r), z, jnp.sin(ang_c)], axis=-1)
    tile = lambda a: jnp.concatenate([a, a], axis=-1)
    return tile(cos), tile(sin_up), tile(sin_dn)


def _rope(x, cos, sin_up, sin_dn):
    outs = []
    for s in range(x.shape[-1] // LANES):
        xs = x[:, s * LANES:(s + 1) * LANES]
        up = pltpu.roll(xs, LANES - 16, 1)
        dn = pltpu.roll(xs, 16, 1)
        outs.append(xs * cos + up * sin_up + dn * sin_dn)
    return outs[0] if len(outs) == 1 else jnp.concatenate(outs, axis=-1)


def _silu(x):
    return x * jax.nn.sigmoid(x)


def _ab_in_kernel(*refs, rope):
    if rope:
        (x_ref, shift_ref, scale_ref, w_ref, ng_ref, nb_ref, cos_ref, su_ref, sd_ref,
         ta_ref, vn_ref, q_ref, k_ref, v_ref, sg_ref) = refs
    else:
        (x_ref, shift_ref, scale_ref, w_ref, ng_ref, nb_ref,
         ta_ref, vn_ref, q_ref, k_ref, v_ref, sg_ref) = refs
    u = (x_ref[0] * (1.0 + scale_ref[0]) + shift_ref[0]).astype(BF16)

    def seg(s):
        return jnp.dot(u, w_ref[:, s * BRANCH:(s + 1) * BRANCH], preferred_element_type=F32)

    ta_ref[0] = (jax.nn.gelu(seg(0)) * _silu(seg(2))).astype(BF16)
    vn_ref[0] = _layer_norm(jax.nn.gelu(seg(1)), ng_ref[...], nb_ref[...]).astype(BF16)
    q = seg(3)
    k = seg(4)
    if rope:
        tabs = (cos_ref[...], su_ref[...], sd_ref[...])
        q = _rope(q, *tabs)
        k = _rope(k, *tabs)
    q_ref[0] = (q * (HEAD_DIM ** -0.5)).astype(BF16)
    k_ref[0] = k.astype(BF16)
    v_ref[0] = seg(5).astype(BF16)
    sg_ref[0] = _silu(seg(6)).astype(BF16)


def _ab_in(h, mod, mod_row, w_in, ng, nb, tables, tm):
    bsz, n, _ = h.shape
    rope = tables is not None
    in_specs = [pl.BlockSpec((1, tm, D_MODEL), lambda b, i: (b, i, 0))]
    in_specs += _mod_specs(mod_row, (0, 1))
    in_specs += [
        pl.BlockSpec(w_in.shape, lambda b, i: (0, 0)),
        pl.BlockSpec((1, BRANCH), lambda b, i: (0, 0)),
        pl.BlockSpec((1, BRANCH), lambda b, i: (0, 0)),
    ]
    args = [h, mod, mod, w_in, ng, nb]
    if rope:
        in_specs += [pl.BlockSpec((tm, LANES), lambda b, i: (i, 0))] * 3
        args += list(tables)
    out_spec = pl.BlockSpec((1, tm, BRANCH), lambda b, i: (b, i, 0))
    out_shape = jax.ShapeDtypeStruct((bsz, n, BRANCH), BF16)
    return pl.pallas_call(
        functools.partial(_ab_in_kernel, rope=rope),
        grid=(bsz, n // tm),
        in_specs=in_specs,
        out_specs=[out_spec] * 6,
        out_shape=[out_shape] * 6,
        compiler_params=_params(2),
        name="ab_in_rope" if rope else "ab_in_ctx",
    )(*args)


def _diff_attn_kernel(*refs, n_seg, lam_init):
    lv_ref, g_ref, q_ref, sg_ref = refs[:4]
    kv = refs[4:4 + 2 * n_seg]
    o_ref = refs[4 + 2 * n_seg]
    lv = lv_ref[...]
    lam = (jnp.exp(jnp.sum(lv[0:1] * lv[1:2], axis=-1, keepdims=True))
           - jnp.exp(jnp.sum(lv[2:3] * lv[3:4], axis=-1, keepdims=True)) + lam_init)
    q = q_ref[0]
    zero = jnp.zeros_like(q)
    probs = []
    for comp in range(2):
        qc = jnp.where(_lane_mask(q.shape, comp == 0), q, zero)
        ss = [lax.dot_general(qc, kv[2 * j][0], _NT, preferred_element_type=F32) for j in range(n_seg)]
        m = functools.reduce(jnp.maximum, [jnp.max(s, axis=-1, keepdims=True) for s in ss])
        ps = [jnp.exp(s - m) for s in ss]
        l = functools.reduce(jnp.add, [jnp.sum(p, axis=-1, keepdims=True) for p in ps])
        probs.append((ps, l))
    c0 = 1.0 / probs[0][1]
    c1 = lam / probs[1][1]
    o = None
    for j in range(n_seg):
        w = (probs[0][0][j] * c0 - probs[1][0][j] * c1).astype(BF16)
        pv = jnp.dot(w, kv[2 * j + 1][0], preferred_element_type=F32)
        o = pv if o is None else o + pv
    o = o * lax.rsqrt(jnp.mean(o * o, axis=-1, keepdims=True) + RMS_EPS) * g_ref[...] * (1.0 - lam_init)
    o_ref[0] = (o * sg_ref[0].astype(F32)).astype(BF16)


def _diff_attn(lvec, subln_g, q, sg, segs, lam_init, tq):
    bsz, n, _ = q.shape
    heads = BRANCH // LANES
    qspec = pl.BlockSpec((1, tq, LANES), lambda b, h, i: (b, i, h))
    in_specs = [
        pl.BlockSpec(lvec.shape, lambda b, h, i: (0, 0)),
        pl.BlockSpec((1, LANES), lambda b, h, i: (0, 0)),
        qspec, qspec,
    ]
    args = [lvec, subln_g, q, sg]
    for k, v in segs:
        spec = pl.BlockSpec((1, k.shape[1], LANES), lambda b, h, i: (b, 0, h))
        in_specs += [spec, spec]
        args += [k, v]
    return pl.pallas_call(
        functools.partial(_diff_attn_kernel, n_seg=len(segs), lam_init=lam_init),
        grid=(bsz, heads, n // tq),
        in_specs=in_specs,
        out_specs=qspec,
        out_shape=jax.ShapeDtypeStruct((bsz, n, BRANCH), BF16),
        compiler_params=_params(3),
        name=f"diff_attn_{len(segs)}seg",
    )(*args)


def _out_proj_ln(left, right, w_ref, h, gate, lg, lb, alpha):
    y = jnp.dot(left, w_ref[:BRANCH, :], preferred_element_type=F32)
    y = y + jnp.dot(right, w_ref[BRANCH:, :], preferred_element_type=F32)
    return _layer_norm(alpha * h + gate * y, lg, lb)


def _ab_out_kernel(ta_ref, vn_ref, bm_ref, h_ref, gate_ref, ws_ref, bs_ref, w_ref, lg_ref, lb_ref, o_ref,
                   *, alpha):
    tm = ta_ref.shape[1]
    groups = BRANCH // LANES
    rows = []
    for c in range(tm // CHUNK):
        cols = []
        for g in range(groups):
            vc = vn_ref[0, c * CHUNK:(c + 1) * CHUNK, g * LANES:(g + 1) * LANES]
            cols.append(jnp.dot(ws_ref[g], vc, preferred_element_type=F32))
        rows.append(jnp.concatenate(cols, axis=-1) + bs_ref[...])
    mixed = rows[0] if len(rows) == 1 else jnp.concatenate(rows, axis=0)
    a_mix = (ta_ref[0].astype(F32) * mixed).astype(BF16)
    o_ref[0] = _out_proj_ln(a_mix, bm_ref[0], w_ref, h_ref[0], gate_ref[0], lg_ref[...], lb_ref[...], alpha)


def _ab_out(ta, vn, bm, h, mod, mod_row, ws, bs2d, w_out, lg, lb, alpha, tm):
    bsz, n, _ = h.shape
    bspec = pl.BlockSpec((1, tm, BRANCH), lambda b, i: (b, i, 0))
    hspec = pl.BlockSpec((1, tm, D_MODEL), lambda b, i: (b, i, 0))
    in_specs = [bspec, bspec, bspec, hspec] + _mod_specs(mod_row, (2,)) + [
        pl.BlockSpec(ws.shape, lambda b, i: (0, 0, 0)),
        pl.BlockSpec(bs2d.shape, lambda b, i: (0, 0)),
        pl.BlockSpec(w_out.shape, lambda b, i: (0, 0)),
        pl.BlockSpec((1, D_MODEL), lambda b, i: (0, 0)),
        pl.BlockSpec((1, D_MODEL), lambda b, i: (0, 0)),
    ]
    return pl.pallas_call(
        functools.partial(_ab_out_kernel, alpha=alpha),
        grid=(bsz, n // tm),
        in_specs=in_specs,
        out_specs=hspec,
        out_shape=jax.ShapeDtypeStruct((bsz, n, D_MODEL), F32),
        compiler_params=_params(2),
        name="ab_out",
    )(ta, vn, bm, h, mod, ws, bs2d, w_out, lg, lb)


def _cd_in_kernel(x_ref, shift_ref, scale_ref, w_ref, cos_ref, su_ref, sd_ref,
                  glu_ref, cg_ref, q_ref, k_ref, ks_ref, v_ref, vs_ref, dg_ref):
    u = (x_ref[0] * (1.0 + scale_ref[0]) + shift_ref[0]).astype(BF16)

    def cols(lo, width):
        return jnp.dot(u, w_ref[:, lo:lo + width], preferred_element_type=F32)

    glu_ref[0] = cols(0, BRANCH) * jax.nn.sigmoid(cols(BRANCH, BRANCH))
    cg_ref[0] = _silu(cols(2 * BRANCH, BRANCH)).astype(BF16)
    tabs = (cos_ref[...], su_ref[...], sd_ref[...])
    q_ref[0] = (_rope(cols(3 * BRANCH, BRANCH), *tabs) * (HEAD_DIM ** -0.5)).astype(BF16)
    k = _rope(cols(4 * BRANCH, LANES), *tabs)
    v = cols(4 * BRANCH + LANES, LANES)
    k_ref[0] = k.astype(BF16)
    ks_ref[0] = pltpu.roll(k, HEAD_DIM, 1).astype(BF16)
    v_ref[0] = v.astype(BF16)
    vs_ref[0] = pltpu.roll(v, HEAD_DIM, 1).astype(BF16)
    dg_ref[0] = _silu(cols(4 * BRANCH + 2 * LANES, BRANCH)).astype(BF16)


def _cd_in(h, mod, mod_row, w_in, tables, tm):
    bsz, n, _ = h.shape
    in_specs = [pl.BlockSpec((1, tm, D_MODEL), lambda b, i: (b, i, 0))] + _mod_specs(mod_row, (0, 1))
    in_specs += [pl.BlockSpec(w_in.shape, lambda b, i: (0, 0))]
    in_specs += [pl.BlockSpec((tm, LANES), lambda b, i: (i, 0))] * 3
    wide = pl.BlockSpec((1, tm, BRANCH), lambda b, i: (b, i, 0))
    narrow = pl.BlockSpec((1, tm, LANES), lambda b, i: (b, i, 0))
    wide_bf = jax.ShapeDtypeStruct((bsz, n, BRANCH), BF16)
    narrow_bf = jax.ShapeDtypeStruct((bsz, n, LANES), BF16)
    return pl.pallas_call(
        _cd_in_kernel,
        grid=(bsz, n // tm),
        in_specs=in_specs,
        out_specs=[wide, wide, wide, narrow, narrow, narrow, narrow, wide],
        out_shape=[jax.ShapeDtypeStruct((bsz, n, BRANCH), F32), wide_bf, wide_bf,
                   narrow_bf, narrow_bf, narrow_bf, narrow_bf, wide_bf],
        compiler_params=_params(2),
        name="cd_in",
    )(h, mod, mod, w_in, *tables)


def _cd_ctx_kv_kernel(x_ref, shift_ref, scale_ref, w_ref, k_ref, ks_ref, v_ref, vs_ref):
    u = (x_ref[0] * (1.0 + scale_ref[0]) + shift_ref[0]).astype(BF16)
    kv = jnp.dot(u, w_ref[...], preferred_element_type=F32)
    k = kv[:, :LANES]
    v = kv[:, LANES:]
    k_ref[0] = k.astype(BF16)
    ks_ref[0] = pltpu.roll(k, HEAD_DIM, 1).astype(BF16)
    v_ref[0] = v.astype(BF16)
    vs_ref[0] = pltpu.roll(v, HEAD_DIM, 1).astype(BF16)


def _cd_ctx_kv(h, mod, mod_row, w_kv):
    bsz, n, _ = h.shape
    in_specs = [pl.BlockSpec((1, n, D_MODEL), lambda b: (b, 0, 0))]
    in_specs += [pl.BlockSpec((1, 1, D_MODEL), functools.partial(lambda b, p: (mod_row(b), 0, p), p=p))
                 for p in (0, 1)]
    in_specs += [pl.BlockSpec(w_kv.shape, lambda b: (0, 0))]
    spec = pl.BlockSpec((1, n, LANES), lambda b: (b, 0, 0))
    shape = jax.ShapeDtypeStruct((bsz, n, LANES), BF16)
    return pl.pallas_call(
        _cd_ctx_kv_kernel,
        grid=(bsz,),
        in_specs=in_specs,
        out_specs=[spec] * 4,
        out_shape=[shape] * 4,
        compiler_params=_params(1),
        name="cd_ctx_kv",
    )(h, mod, mod, w_kv)


def _swa_kernel(sink_ref, q_ref, dg_ref,
                kp_ref, kc_ref, kn_ref, ksp_ref, ksc_ref, ksn_ref,
                vp_ref, vc_ref, vn_ref, vsp_ref, vsc_ref, vsn_ref,
                ck_ref, cks_ref, cv_ref, cvs_ref, o_ref):
    i = pl.program_id(1)
    nb = pl.num_programs(1)
    tq = q_ref.shape[1]
    r = lax.broadcasted_iota(jnp.int32, (tq, CHUNK), 0)
    c = lax.broadcasted_iota(jnp.int32, (tq, CHUNK), 1)
    mask_prev = c >= r + jnp.where(i > 0, 0, CHUNK)
    mask_next = c <= r - jnp.where(i < nb - 1, 0, CHUNK)
    lo = _lane_mask((1, LANES), True)
    k_lo = ((kp_ref, kc_ref, kn_ref, ck_ref), (ksp_ref, ksc_ref, ksn_ref, cks_ref))
    v_lo = ((vp_ref, vc_ref, vn_ref, cv_ref), (vsp_ref, vsc_ref, vsn_ref, cvs_ref))
    out = []
    for slab in range(BRANCH // LANES):
        kvh = slab // 2
        q = q_ref[0, :, slab * LANES:(slab + 1) * LANES]
        zero = jnp.zeros_like(q)
        acc = None
        inv_l = []
        for half in range(2):
            head = 2 * slab + half
            qh = jnp.where(lo if half == 0 else ~lo, q, zero)
            ks = k_lo[kvh] if half == 0 else k_lo[1 - kvh]
            vs = v_lo[kvh] if half == 0 else v_lo[1 - kvh]
            ss = [lax.dot_general(qh, kr[0], _NT, preferred_element_type=F32) for kr in ks]
            ss[0] = jnp.where(mask_prev, ss[0], NEG_INF)
            ss[2] = jnp.where(mask_next, ss[2], NEG_INF)
            sink = sink_ref[head]
            m = functools.reduce(jnp.maximum, [jnp.max(s, axis=-1, keepdims=True) for s in ss])
            m = jnp.maximum(m, sink)
            ps = [jnp.exp(s - m) for s in ss]
            l = functools.reduce(jnp.add, [jnp.sum(p, axis=-1, keepdims=True) for p in ps])
            l = l + jnp.exp(sink - m)
            inv_l.append(1.0 / l)
            keep = lo if half == 0 else ~lo
            for p, vr in zip(ps, vs):
                vh = jnp.where(keep, vr[0], jnp.zeros_like(vr[0]))
                pv = jnp.dot(p.astype(BF16), vh, preferred_element_type=F32)
                acc = pv if acc is None else acc + pv
        out.append(acc * jnp.where(lo, inv_l[0], inv_l[1]))
    o = jnp.concatenate(out, axis=-1)
    o_ref[0] = (o * dg_ref[0].astype(F32)).astype(BF16)


def _swa(sink, q, dg, k, ks, v, vs, ck, cks, cv, cvs):
    bsz, n, _ = q.shape
    nb = n // CHUNK
    qspec = pl.BlockSpec((1, CHUNK, BRANCH), lambda b, i: (b, i, 0))
    prev = pl.BlockSpec((1, CHUNK, LANES), lambda b, i: (b, jnp.maximum(i - 1, 0), 0))
    cur = pl.BlockSpec((1, CHUNK, LANES), lambda b, i: (b, i, 0))
    nxt = pl.BlockSpec((1, CHUNK, LANES), lambda b, i: (b, jnp.minimum(i + 1, nb - 1), 0))
    cspec = pl.BlockSpec((1, ck.shape[1], LANES), lambda b, i: (b, 0, 0))
    in_specs = [pl.BlockSpec(memory_space=pltpu.SMEM), qspec, qspec] + [prev, cur, nxt] * 4 + [cspec] * 4
    return pl.pallas_call(
        _swa_kernel,
        grid=(bsz, nb),
        in_specs=in_specs,
        out_specs=qspec,
        out_shape=jax.ShapeDtypeStruct((bsz, n, BRANCH), BF16),
        compiler_params=_params(2),
        name="swa",
    )(sink, q, dg, k, k, k, ks, ks, ks, v, v, v, vs, vs, vs, ck, cks, cv, cvs)


def _cd_out_kernel(glu_ref, gp_ref, gn_ref, cg_ref, dm_ref, h_ref, gate_ref, dw_ref, db_ref, cng_ref, cnb_ref,
                   w_ref, lg_ref, lb_ref, o_ref, ext_ref, *, alpha):
    i = pl.program_id(1)
    nb = pl.num_programs(1)
    tm = glu_ref.shape[1]
    ext_ref[0:CONV_HALO, :] = jnp.where(i > 0, gp_ref[0], 0.0)
    ext_ref[CONV_HALO:CONV_HALO + tm, :] = glu_ref[0]
    ext_ref[CONV_HALO + tm:, :] = jnp.where(i < nb - 1, gn_ref[0], 0.0)
    off = CONV_HALO - C_KERNEL // 2
    y = None
    for j in range(C_KERNEL):
        term = ext_ref[off + j:off + j + tm, :] * dw_ref[j:j + 1, :]
        y = term if y is None else y + term
    y = y + db_ref[...]
    c_lat = _silu(_layer_norm(y, cng_ref[...], cnb_ref[...]))
    c_mix = (c_lat * cg_ref[0].astype(F32)).astype(BF16)
    o_ref[0] = _out_proj_ln(c_mix, dm_ref[0], w_ref, h_ref[0], gate_ref[0], lg_ref[...], lb_ref[...], alpha)


def _cd_out(glu, cg, dm, h, mod, mod_row, dw_w, dw_b, cng, cnb, w_out, lg, lb, alpha, tm):
    bsz, n, _ = h.shape
    per = tm // CONV_HALO
    last = n // CONV_HALO - 1
    bspec = pl.BlockSpec((1, tm, BRANCH), lambda b, i: (b, i, 0))
    hspec = pl.BlockSpec((1, tm, D_MODEL), lambda b, i: (b, i, 0))
    vec = lambda width: pl.BlockSpec((1, width), lambda b, i: (0, 0))
    in_specs = [
        bspec,
        pl.BlockSpec((1, CONV_HALO, BRANCH), lambda b, i: (b, jnp.maximum(i * per - 1, 0), 0)),
        pl.BlockSpec((1, CONV_HALO, BRANCH), lambda b, i: (b, jnp.minimum((i + 1) * per, last), 0)),
        bspec, bspec, hspec,
    ] + _mod_specs(mod_row, (2,)) + [
        pl.BlockSpec(dw_w.shape, lambda b, i: (0, 0)),
        vec(BRANCH), vec(BRANCH), vec(BRANCH),
        pl.BlockSpec(w_out.shape, lambda b, i: (0, 0)),
        vec(D_MODEL), vec(D_MODEL),
    ]
    return pl.pallas_call(
        functools.partial(_cd_out_kernel, alpha=alpha),
        grid=(bsz, n // tm),
        in_specs=in_specs,
        out_specs=hspec,
        out_shape=jax.ShapeDtypeStruct((bsz, n, D_MODEL), F32),
        scratch_shapes=[pltpu.VMEM((tm + 2 * CONV_HALO, BRANCH), F32)],
        compiler_params=_params(2),
        name="cd_out",
    )(glu, glu, glu, cg, dm, h, mod, dw_w, dw_b, cng, cnb, w_out, lg, lb)


def kernel(x, c, ctx, c_ctx, mod_w, mod_b, ln_g, ln_b, ab_w_in, ab_w_out, a_w_s, a_b_s, a_norm_g, a_norm_b,
           b_lq1, b_lk1, b_lq2, b_lk2, b_subln_g, cd_w_in, cd_w_out, c_dw_w, c_dw_b, c_norm_g, c_norm_b, d_sink):
    bsz, n, _ = x.shape
    assert DEPTH == 2 and bsz + 1 <= MOD_ROWS
    alpha = (2.0 * DEPTH) ** 0.25
    tables = _rope_tables(n)

    cc = jnp.concatenate([c, c_ctx[None, :], jnp.zeros((MOD_ROWS - bsz - 1, D_MODEL), F32)], axis=0)
    mod = _modulation(cc, mod_w, mod_b).reshape(DEPTH * MOD_ROWS, 1, 3 * D_MODEL)
    lat_row = lambda layer: (lambda b: layer * MOD_ROWS + b)
    ctx_row = lambda layer: (lambda b: layer * MOD_ROWS + bsz)
    row2d = lambda a: a.reshape(1, -1)

    w_in0 = ab_w_in[0].astype(BF16)
    w_out0 = ab_w_out[0].astype(BF16)
    ws = a_w_s[0].astype(BF16)
    bs2d = jnp.repeat(a_b_s[0].T, LANES, axis=1)
    ng, nbias = row2d(a_norm_g[0]), row2d(a_norm_b[0])
    lvec = jnp.stack([b_lq1[0], b_lk1[0], b_lq2[0], b_lk2[0]])
    subln = row2d(b_subln_g[0])
    lam_init0 = 0.8 - 0.6 * math.exp(-0.3 * 0)
    lg0, lb0 = row2d(ln_g[0]), row2d(ln_b[0])

    ta, vn, q, k, v, sg = _ab_in(x, mod, lat_row(0), w_in0, ng, nbias, tables, tm=512)
    cta, cvn, cq, ck, cv, csg = _ab_in(ctx, mod, ctx_row(0), w_in0, ng, nbias, None, tm=CTX_LEN)
    bm = _diff_attn(lvec, subln, q, sg, [(ck, cv), (k, v)], lam_init0, tq=256)
    cbm = _diff_attn(lvec, subln, cq, csg, [(ck, cv)], lam_init0, tq=CTX_LEN)
    h1 = _ab_out(ta, vn, bm, x, mod, lat_row(0), ws, bs2d, w_out0, lg0, lb0, alpha, tm=512)
    hc1 = _ab_out(cta, cvn, cbm, ctx, mod, ctx_row(0), ws, bs2d, w_out0, lg0, lb0, alpha, tm=CTX_LEN)

    w_in1 = cd_w_in[0].astype(BF16)
    w_out1 = cd_w_out[0].astype(BF16)
    kv_lo = 4 * BRANCH
    glu, cg, dq, dk, dks, dv, dvs, dg = _cd_in(h1, mod, lat_row(1), w_in1, tables, tm=512)
    ck1, cks1, cv1, cvs1 = _cd_ctx_kv(hc1, mod, ctx_row(1), w_in1[:, kv_lo:kv_lo + 2 * LANES])
    dm = _swa(d_sink[0], dq, dg, dk, dks, dv, dvs, ck1, cks1, cv1, cvs1)
    return _cd_out(glu, cg, dm, h1, mod, lat_row(1), c_dw_w[0], row2d(c_dw_b[0]), row2d(c_norm_g[0]),
                   row2d(c_norm_b[0]), w_out1, row2d(ln_g[1]), row2d(ln_b[1]), alpha, tm=512)
```

```python
import functools
import math

import jax
import jax.numpy as jnp
from jax import lax
from jax.experimental import pallas as pl
from jax.experimental.pallas import tpu as pltpu

F32 = jnp.float32
BF16 = jnp.bfloat16

D_MODEL = 1024
DEPTH = 2
CTX_LEN = 256
GRID_W = 64
ROPE_THETA = 10000.0
LN_EPS = 1e-6
RMS_EPS = 1e-5
NEG_INF = -1e30
HEAD_DIM = 64
BRANCH = 512
CHUNK = 128
C_KERNEL = 31
CONV_HALO = 16
MOD_ROWS = 24
LANES = 128

VMEM_LIMIT = 56 * 1024 * 1024
LOG2E = math.log2(math.e)
Q_SCALE = HEAD_DIM ** -0.5 * LOG2E
KEY_CHUNK = 256
CONV_ROWS = 64

_NT = (((1,), (1,)), ((), ()))


def _params(n_axes, vmem=VMEM_LIMIT):
    return pltpu.CompilerParams(dimension_semantics=("arbitrary",) * n_axes, vmem_limit_bytes=vmem)


def _layer_norm(x, g, b):
    mu = jnp.mean(x, axis=-1, keepdims=True)
    xc = x - mu
    var = jnp.mean(xc * xc, axis=-1, keepdims=True)
    return xc * lax.rsqrt(var + LN_EPS) * g + b


def _lane_mask(shape, lo):
    lane = lax.broadcasted_iota(jnp.int32, shape, len(shape) - 1) % LANES
    return (lane < HEAD_DIM) if lo else (lane >= HEAD_DIM)


def _mod_kernel(c_ref, w_ref, b_ref, o_ref):
    c = c_ref[...]
    a = c * jax.nn.sigmoid(c)
    a_hi = a.astype(BF16)
    a_lo = (a - a_hi.astype(F32)).astype(BF16)
    w = w_ref[0]
    w_hi = w.astype(BF16)
    w_lo = (w - w_hi.astype(F32)).astype(BF16)
    acc = jnp.dot(a_hi, w_hi, preferred_element_type=F32)
    acc = acc + jnp.dot(a_hi, w_lo, preferred_element_type=F32)
    acc = acc + jnp.dot(a_lo, w_hi, preferred_element_type=F32)
    o_ref[0] = acc + b_ref[0]


def _modulation(cc, mod_w, mod_b):
    tn = 768
    n_out = 3 * D_MODEL
    return pl.pallas_call(
        _mod_kernel,
        grid=(DEPTH, n_out // tn),
        in_specs=[
            pl.BlockSpec((MOD_ROWS, D_MODEL), lambda l, j: (0, 0)),
            pl.BlockSpec((1, D_MODEL, tn), lambda l, j: (l, 0, j)),
            pl.BlockSpec((1, 1, tn), lambda l, j: (l, 0, j)),
        ],
        out_specs=pl.BlockSpec((1, MOD_ROWS, tn), lambda l, j: (l, 0, j)),
        out_shape=jax.ShapeDtypeStruct((DEPTH, MOD_ROWS, n_out), F32),
        compiler_params=_params(2),
        name="modulation",
    )(cc, mod_w, mod_b.reshape(DEPTH, 1, n_out))


def _mod_specs(row_fn, parts):
    return [pl.BlockSpec((1, 1, D_MODEL), functools.partial(lambda b, i, p: (row_fn(b), 0, p), p=p))
            for p in parts]


def _rope_tables(n):
    m = HEAD_DIM // 4
    inv = ROPE_THETA ** (-jnp.arange(m, dtype=F32) / m)
    t = jnp.arange(n, dtype=jnp.int32)
    ang_r = (t // GRID_W).astype(F32)[:, None] * inv
    ang_c = (t % GRID_W).astype(F32)[:, None] * inv
    z = jnp.zeros_like(ang_r)
    cos = jnp.concatenate([jnp.cos(ang_r)] * 2 + [jnp.cos(ang_c)] * 2, axis=-1)
    sin_up = jnp.concatenate([-jnp.sin(ang_r), z, -jnp.sin(ang_c), z], axis=-1)
    sin_dn = jnp.concatenate([z, jnp.sin(ang_r), z, jnp.sin(ang_c)], axis=-1)
    tile = lambda a: jnp.concatenate([a, a], axis=-1)
    return tile(cos), tile(sin_up), tile(sin_dn)


def _rope(x, cos, sin_up, sin_dn):
    outs = []
    for s in range(x.shape[-1] // LANES):
        xs = x[:, s * LANES:(s + 1) * LANES]
        up = pltpu.roll(xs, LANES - 16, 1)
        dn = pltpu.roll(xs, 16, 1)
        outs.append(xs * cos + up * sin_up + dn * sin_dn)
    return outs[0] if len(outs) == 1 else jnp.concatenate(outs, axis=-1)


def _silu(x):
    return x * jax.nn.sigmoid(x)


def _ab_in_kernel(*refs, rope):
    if rope:
        (x_ref, shift_ref, scale_ref, w_ref, ng_ref, nb_ref, cos_ref, su_ref, sd_ref,
         ta_ref, vn_ref, q_ref, k_ref, v_ref, sg_ref) = refs
    else:
        (x_ref, shift_ref, scale_ref, w_ref, ng_ref, nb_ref,
         ta_ref, vn_ref, q_ref, k_ref, v_ref, sg_ref) = refs
    u = (x_ref[0] * (1.0 + scale_ref[0]) + shift_ref[0]).astype(BF16)

    def seg(s):
        return jnp.dot(u, w_ref[:, s * BRANCH:(s + 1) * BRANCH], preferred_element_type=F32)

    ta_ref[0] = (jax.nn.gelu(seg(0)) * _silu(seg(2))).astype(BF16)
    vn_ref[0] = _layer_norm(jax.nn.gelu(seg(1)), ng_ref[...], nb_ref[...]).astype(BF16)
    q = seg(3)
    k = seg(4)
    if rope:
        tabs = (cos_ref[...], su_ref[...], sd_ref[...])
        q = _rope(q, *tabs)
        k = _rope(k, *tabs)
    q_ref[0] = (q * Q_SCALE).astype(BF16)
    k_ref[0] = k.astype(BF16)
    v_ref[0] = seg(5).astype(BF16)
    sg_ref[0] = _silu(seg(6)).astype(BF16)


def _ab_in(h, mod, mod_row, w_in, ng, nb, tables, tm):
    bsz, n, _ = h.shape
    rope = tables is not None
    in_specs = [pl.BlockSpec((1, tm, D_MODEL), lambda b, i: (b, i, 0))]
    in_specs += _mod_specs(mod_row, (0, 1))
    in_specs += [
        pl.BlockSpec(w_in.shape, lambda b, i: (0, 0)),
        pl.BlockSpec((1, BRANCH), lambda b, i: (0, 0)),
        pl.BlockSpec((1, BRANCH), lambda b, i: (0, 0)),
    ]
    args = [h, mod, mod, w_in, ng, nb]
    if rope:
        in_specs += [pl.BlockSpec((tm, LANES), lambda b, i: (i, 0))] * 3
        args += list(tables)
    out_spec = pl.BlockSpec((1, tm, BRANCH), lambda b, i: (b, i, 0))
    out_shape = jax.ShapeDtypeStruct((bsz, n, BRANCH), BF16)
    return pl.pallas_call(
        functools.partial(_ab_in_kernel, rope=rope),
        grid=(bsz, n // tm),
        in_specs=in_specs,
        out_specs=[out_spec] * 6,
        out_shape=[out_shape] * 6,
        compiler_params=_params(2),
        name="ab_in_rope" if rope else "ab_in_ctx",
    )(*args)


def _diff_attn_kernel(*refs, seg_rows, lam_init):
    n_seg = len(seg_rows)
    lv_ref, g_ref, q_ref, sg_ref = refs[:4]
    kv = refs[4:4 + 2 * n_seg]
    o_ref = refs[4 + 2 * n_seg]
    vext_ref = refs[5 + 2 * n_seg]
    heads = BRANCH // LANES

    @pl.when(pl.program_id(1) == 0)
    def _fill_values():
        for h in range(heads):
            base = 0
            for j, rows in enumerate(seg_rows):
                vext_ref[h, base:base + rows, :LANES] = kv[2 * j + 1][0, :, h * LANES:(h + 1) * LANES]
                base += rows
            vext_ref[h, :, LANES:] = jnp.ones((base, LANES), BF16)

    lv = lv_ref[...]
    lam = (jnp.exp(jnp.sum(lv[0:1] * lv[1:2], axis=-1, keepdims=True))
           - jnp.exp(jnp.sum(lv[2:3] * lv[3:4], axis=-1, keepdims=True)) + lam_init)
    tq = q_ref.shape[1]
    lo = _lane_mask((tq, LANES), True)
    for h in range(heads):
        cols = slice(h * LANES, (h + 1) * LANES)
        q = q_ref[0, :, cols]
        zero = jnp.zeros_like(q)
        qs = jnp.concatenate([jnp.where(lo, q, zero), jnp.where(lo, zero, q)], axis=0)
        m = acc = None
        base = 0
        for j, rows in enumerate(seg_rows):
            for c0 in range(0, rows, KEY_CHUNK):
                n = min(KEY_CHUNK, rows - c0)
                s = lax.dot_general(qs, kv[2 * j][0, c0:c0 + n, cols], _NT, preferred_element_type=F32)
                smax = jnp.max(s, axis=-1, keepdims=True)
                m_new = smax if m is None else jnp.maximum(m, smax)
                p = jnp.exp2(s - m_new).astype(BF16)
                pv = jnp.dot(p, vext_ref[h, base + c0:base + c0 + n, :], preferred_element_type=F32)
                acc = pv if acc is None else jnp.exp2(m - m_new) * acc + pv
                m = m_new
            base += rows
        on = acc[:, :LANES] * (1.0 / acc[:, LANES:])
        o = on[:tq] - lam * on[tq:]
        o = o * lax.rsqrt(jnp.mean(o * o, axis=-1, keepdims=True) + RMS_EPS) * g_ref[...] * (1.0 - lam_init)
        o_ref[0, :, cols] = (o * sg_ref[0, :, cols].astype(F32)).astype(BF16)


def _diff_attn(lvec, subln_g, q, sg, segs, lam_init, tq):
    bsz, n, _ = q.shape
    qspec = pl.BlockSpec((1, tq, BRANCH), lambda b, i: (b, i, 0))
    in_specs = [
        pl.BlockSpec(lvec.shape, lambda b, i: (0, 0)),
        pl.BlockSpec((1, LANES), lambda b, i: (0, 0)),
        qspec, qspec,
    ]
    args = [lvec, subln_g, q, sg]
    for k, v in segs:
        spec = pl.BlockSpec((1, k.shape[1], BRANCH), lambda b, i: (b, 0, 0))
        in_specs += [spec, spec]
        args += [k, v]
    seg_rows = tuple(k.shape[1] for k, _ in segs)
    return pl.pallas_call(
        functools.partial(_diff_attn_kernel, seg_rows=seg_rows, lam_init=lam_init),
        grid=(bsz, n // tq),
        in_specs=in_specs,
        out_specs=qspec,
        out_shape=jax.ShapeDtypeStruct((bsz, n, BRANCH), BF16),
        scratch_shapes=[pltpu.VMEM((BRANCH // LANES, sum(seg_rows), 2 * LANES), BF16)],
        compiler_params=_params(2),
        name=f"diff_attn_{len(segs)}seg",
    )(*args)


def _out_proj_ln(left, right, w_ref, h, gate, lg, lb, alpha):
    y = jnp.dot(left, w_ref[:BRANCH, :], preferred_element_type=F32)
    y = y + jnp.dot(right, w_ref[BRANCH:, :], preferred_element_type=F32)
    return _layer_norm(alpha * h + gate * y, lg, lb)


def _ab_out_kernel(ta_ref, vn_ref, bm_ref, h_ref, gate_ref, ws_ref, bs_ref, w_ref, lg_ref, lb_ref, o_ref,
                   *, alpha):
    tm = ta_ref.shape[1]
    groups = BRANCH // LANES
    rows = []
    for c in range(tm // CHUNK):
        cols = []
        for g in range(groups):
            vc = vn_ref[0, c * CHUNK:(c + 1) * CHUNK, g * LANES:(g + 1) * LANES]
            cols.append(jnp.dot(ws_ref[g], vc, preferred_element_type=F32))
        rows.append(jnp.concatenate(cols, axis=-1) + bs_ref[...])
    mixed = rows[0] if len(rows) == 1 else jnp.concatenate(rows, axis=0)
    a_mix = (ta_ref[0].astype(F32) * mixed).astype(BF16)
    o_ref[0] = _out_proj_ln(a_mix, bm_ref[0], w_ref, h_ref[0], gate_ref[0], lg_ref[...], lb_ref[...], alpha)


def _ab_out(ta, vn, bm, h, mod, mod_row, ws, bs2d, w_out, lg, lb, alpha, tm):
    bsz, n, _ = h.shape
    bspec = pl.BlockSpec((1, tm, BRANCH), lambda b, i: (b, i, 0))
    hspec = pl.BlockSpec((1, tm, D_MODEL), lambda b, i: (b, i, 0))
    in_specs = [bspec, bspec, bspec, hspec] + _mod_specs(mod_row, (2,)) + [
        pl.BlockSpec(ws.shape, lambda b, i: (0, 0, 0)),
        pl.BlockSpec(bs2d.shape, lambda b, i: (0, 0)),
        pl.BlockSpec(w_out.shape, lambda b, i: (0, 0)),
        pl.BlockSpec((1, D_MODEL), lambda b, i: (0, 0)),
        pl.BlockSpec((1, D_MODEL), lambda b, i: (0, 0)),
    ]
    return pl.pallas_call(
        functools.partial(_ab_out_kernel, alpha=alpha),
        grid=(bsz, n // tm),
        in_specs=in_specs,
        out_specs=hspec,
        out_shape=jax.ShapeDtypeStruct((bsz, n, D_MODEL), F32),
        compiler_params=_params(2),
        name="ab_out",
    )(ta, vn, bm, h, mod, ws, bs2d, w_out, lg, lb)


def _cd_in_kernel(x_ref, shift_ref, scale_ref, w_ref, cos_ref, su_ref, sd_ref,
                  glu_ref, cg_ref, q_ref, k_ref, ks_ref, v_ref, vs_ref, dg_ref):
    u = (x_ref[0] * (1.0 + scale_ref[0]) + shift_ref[0]).astype(BF16)

    def cols(lo, width):
        return jnp.dot(u, w_ref[:, lo:lo + width], preferred_element_type=F32)

    glu_ref[0] = cols(0, BRANCH) * jax.nn.sigmoid(cols(BRANCH, BRANCH))
    cg_ref[0] = _silu(cols(2 * BRANCH, BRANCH)).astype(BF16)
    tabs = (cos_ref[...], su_ref[...], sd_ref[...])
    q_ref[0] = (_rope(cols(3 * BRANCH, BRANCH), *tabs) * Q_SCALE).astype(BF16)
    k = _rope(cols(4 * BRANCH, LANES), *tabs)
    v = cols(4 * BRANCH + LANES, LANES)
    k_ref[0] = k.astype(BF16)
    ks_ref[0] = pltpu.roll(k, HEAD_DIM, 1).astype(BF16)
    v_ref[0] = v.astype(BF16)
    vs_ref[0] = pltpu.roll(v, HEAD_DIM, 1).astype(BF16)
    dg_ref[0] = _silu(cols(4 * BRANCH + 2 * LANES, BRANCH)).astype(BF16)


def _cd_in(h, mod, mod_row, w_in, tables, tm):
    bsz, n, _ = h.shape
    in_specs = [pl.BlockSpec((1, tm, D_MODEL), lambda b, i: (b, i, 0))] + _mod_specs(mod_row, (0, 1))
    in_specs += [pl.BlockSpec(w_in.shape, lambda b, i: (0, 0))]
    in_specs += [pl.BlockSpec((tm, LANES), lambda b, i: (i, 0))] * 3
    wide = pl.BlockSpec((1, tm, BRANCH), lambda b, i: (b, i, 0))
    narrow = pl.BlockSpec((1, tm, LANES), lambda b, i: (b, i, 0))
    wide_bf = jax.ShapeDtypeStruct((bsz, n, BRANCH), BF16)
    narrow_bf = jax.ShapeDtypeStruct((bsz, n, LANES), BF16)
    return pl.pallas_call(
        _cd_in_kernel,
        grid=(bsz, n // tm),
        in_specs=in_specs,
        out_specs=[wide, wide, wide, narrow, narrow, narrow, narrow, wide],
        out_shape=[jax.ShapeDtypeStruct((bsz, n, BRANCH), F32), wide_bf, wide_bf,
                   narrow_bf, narrow_bf, narrow_bf, narrow_bf, wide_bf],
        compiler_params=_params(2),
        name="cd_in",
    )(h, mod, mod, w_in, *tables)


def _cd_ctx_kv_kernel(x_ref, shift_ref, scale_ref, w_ref, k_ref, ks_ref, v_ref, vs_ref):
    u = (x_ref[0] * (1.0 + scale_ref[0]) + shift_ref[0]).astype(BF16)
    kv = jnp.dot(u, w_ref[...], preferred_element_type=F32)
    k = kv[:, :LANES]
    v = kv[:, LANES:]
    k_ref[0] = k.astype(BF16)
    ks_ref[0] = pltpu.roll(k, HEAD_DIM, 1).astype(BF16)
    v_ref[0] = v.astype(BF16)
    vs_ref[0] = pltpu.roll(v, HEAD_DIM, 1).astype(BF16)


def _cd_ctx_kv(h, mod, mod_row, w_kv):
    bsz, n, _ = h.shape
    in_specs = [pl.BlockSpec((1, n, D_MODEL), lambda b: (b, 0, 0))]
    in_specs += [pl.BlockSpec((1, 1, D_MODEL), functools.partial(lambda b, p: (mod_row(b), 0, p), p=p))
                 for p in (0, 1)]
    in_specs += [pl.BlockSpec(w_kv.shape, lambda b: (0, 0))]
    spec = pl.BlockSpec((1, n, LANES), lambda b: (b, 0, 0))
    shape = jax.ShapeDtypeStruct((bsz, n, LANES), BF16)
    return pl.pallas_call(
        _cd_ctx_kv_kernel,
        grid=(bsz,),
        in_specs=in_specs,
        out_specs=[spec] * 4,
        out_shape=[shape] * 4,
        compiler_params=_params(1),
        name="cd_ctx_kv",
    )(h, mod, mod, w_kv)


def _swa_kernel(sink_ref, q_ref, dg_ref,
                kp_ref, kc_ref, kn_ref, ksp_ref, ksc_ref, ksn_ref,
                vp_ref, vc_ref, vn_ref, vsp_ref, vsc_ref, vsn_ref,
                ck_ref, cks_ref, cv_ref, cvs_ref, o_ref):
    i = pl.program_id(1)
    nb = pl.num_programs(1)
    tq = q_ref.shape[1]
    r = lax.broadcasted_iota(jnp.int32, (tq, CHUNK), 0)
    c = lax.broadcasted_iota(jnp.int32, (tq, CHUNK), 1)
    mask_prev = c >= r + jnp.where(i > 0, 0, CHUNK)
    mask_next = c <= r - jnp.where(i < nb - 1, 0, CHUNK)
    n_slab = BRANCH // LANES
    mask_prev = jnp.concatenate([mask_prev] * n_slab, axis=0)
    mask_next = jnp.concatenate([mask_next] * n_slab, axis=0)
    lo = _lane_mask((tq, LANES), True)
    slabs = [q_ref[0, :, j * LANES:(j + 1) * LANES] for j in range(n_slab)]
    zero = jnp.zeros_like(slabs[0])
    q_lo = [jnp.where(lo, q, zero) for q in slabs]
    q_hi = [jnp.where(lo, zero, q) for q in slabs]
    row = lax.broadcasted_iota(jnp.int32, (n_slab * tq, 1), 0)

    groups = (
        (q_lo[:2] + q_hi[2:], (0, 2, 5, 7), (kp_ref, kc_ref, kn_ref, ck_ref), (vp_ref, vc_ref, vn_ref, cv_ref)),
        (q_hi[:2] + q_lo[2:], (1, 3, 4, 6), (ksp_ref, ksc_ref, ksn_ref, cks_ref),
         (vsp_ref, vsc_ref, vsn_ref, cvs_ref)),
    )
    normed = []
    for q_parts, heads, ks, vs in groups:
        qs = jnp.concatenate(q_parts, axis=0)
        sink = jnp.zeros((n_slab * tq, 1), F32)
        for j, head in enumerate(heads):
            sink = jnp.where((row >= j * tq) & (row < (j + 1) * tq), sink_ref[head] * LOG2E, sink)
        ss = [lax.dot_general(qs, kr[0], _NT, preferred_element_type=F32) for kr in ks]
        ss[0] = jnp.where(mask_prev, ss[0], NEG_INF)
        ss[2] = jnp.where(mask_next, ss[2], NEG_INF)
        cols = ss[:3] + [ss[3][:, c0:c0 + LANES] for c0 in range(0, ss[3].shape[1], LANES)]
        m = jnp.max(functools.reduce(jnp.maximum, cols), axis=-1, keepdims=True)
        m = jnp.maximum(m, sink)
        ps = [jnp.exp2(s - m) for s in ss]
        pcols = ps[:3] + [ps[3][:, c0:c0 + LANES] for c0 in range(0, ps[3].shape[1], LANES)]
        l = jnp.sum(functools.reduce(jnp.add, pcols), axis=-1, keepdims=True) + jnp.exp2(sink - m)
        acc = None
        for p, vr in zip(ps, vs):
            pv = jnp.dot(p.astype(BF16), vr[0], preferred_element_type=F32)
            acc = pv if acc is None else acc + pv
        normed.append(acc * (1.0 / l))
    plain, swapped = normed
    out = []
    for j in range(n_slab):
        a = plain[j * tq:(j + 1) * tq]
        b = swapped[j * tq:(j + 1) * tq]
        out.append(jnp.where(lo, a, b) if j < 2 else jnp.where(lo, b, a))
    o = jnp.concatenate(out, axis=-1)
    o_ref[0] = (o * dg_ref[0].astype(F32)).astype(BF16)


def _swa(sink, q, dg, k, ks, v, vs, ck, cks, cv, cvs):
    bsz, n, _ = q.shape
    nb = n // CHUNK
    qspec = pl.BlockSpec((1, CHUNK, BRANCH), lambda b, i: (b, i, 0))
    prev = pl.BlockSpec((1, CHUNK, LANES), lambda b, i: (b, jnp.maximum(i - 1, 0), 0))
    cur = pl.BlockSpec((1, CHUNK, LANES), lambda b, i: (b, i, 0))
    nxt = pl.BlockSpec((1, CHUNK, LANES), lambda b, i: (b, jnp.minimum(i + 1, nb - 1), 0))
    cspec = pl.BlockSpec((1, ck.shape[1], LANES), lambda b, i: (b, 0, 0))
    in_specs = [pl.BlockSpec(memory_space=pltpu.SMEM), qspec, qspec] + [prev, cur, nxt] * 4 + [cspec] * 4
    return pl.pallas_call(
        _swa_kernel,
        grid=(bsz, nb),
        in_specs=in_specs,
        out_specs=qspec,
        out_shape=jax.ShapeDtypeStruct((bsz, n, BRANCH), BF16),
        compiler_params=_params(2),
        name="swa",
    )(sink, q, dg, k, k, k, ks, ks, ks, v, v, v, vs, vs, vs, ck, cks, cv, cvs)


def _cd_out_kernel(glu_ref, gp_ref, gn_ref, cg_ref, dm_ref, h_ref, gate_ref, dw_ref, db_ref, cng_ref, cnb_ref,
                   w_ref, lg_ref, lb_ref, o_ref, ext_ref, y_ref, *, alpha):
    i = pl.program_id(1)
    nb = pl.num_programs(1)
    tm = glu_ref.shape[1]
    ext_ref[0:CONV_HALO, :] = jnp.where(i > 0, gp_ref[0], 0.0)
    ext_ref[CONV_HALO:CONV_HALO + tm, :] = glu_ref[0]
    ext_ref[CONV_HALO + tm:, :] = jnp.where(i < nb - 1, gn_ref[0], 0.0)
    off = CONV_HALO - C_KERNEL // 2
    span = CONV_ROWS + 8

    def conv_rows(t, carry):
        r0 = pl.multiple_of(t * CONV_ROWS, CONV_ROWS)
        for c0 in range(0, BRANCH, LANES):
            win = ext_ref[pl.ds(r0, CONV_ROWS + 2 * CONV_HALO), c0:c0 + LANES]
            y = None
            for s in range(8):
                z = None
                for a in range((off + C_KERNEL + 7) // 8):
                    j = 8 * a + s - off
                    if 0 <= j < C_KERNEL:
                        term = win[8 * a:8 * a + span] * dw_ref[j:j + 1, c0:c0 + LANES]
                        z = term if z is None else z + term
                zs = z[s:s + CONV_ROWS]
                y = zs if y is None else y + zs
            y_ref[pl.ds(r0, CONV_ROWS), c0:c0 + LANES] = y
        return carry

    lax.fori_loop(0, tm // CONV_ROWS, conv_rows, 0)
    y = y_ref[...] + db_ref[...]
    c_lat = _silu(_layer_norm(y, cng_ref[...], cnb_ref[...]))
    c_mix = (c_lat * cg_ref[0].astype(F32)).astype(BF16)
    o_ref[0] = _out_proj_ln(c_mix, dm_ref[0], w_ref, h_ref[0], gate_ref[0], lg_ref[...], lb_ref[...], alpha)


def _cd_out(glu, cg, dm, h, mod, mod_row, dw_w, dw_b, cng, cnb, w_out, lg, lb, alpha, tm):
    bsz, n, _ = h.shape
    per = tm // CONV_HALO
    last = n // CONV_HALO - 1
    bspec = pl.BlockSpec((1, tm, BRANCH), lambda b, i: (b, i, 0))
    hspec = pl.BlockSpec((1, tm, D_MODEL), lambda b, i: (b, i, 0))
    vec = lambda width: pl.BlockSpec((1, width), lambda b, i: (0, 0))
    in_specs = [
        bspec,
        pl.BlockSpec((1, CONV_HALO, BRANCH), lambda b, i: (b, jnp.maximum(i * per - 1, 0), 0)),
        pl.BlockSpec((1, CONV_HALO, BRANCH), lambda b, i: (b, jnp.minimum((i + 1) * per, last), 0)),
        bspec, bspec, hspec,
    ] + _mod_specs(mod_row, (2,)) + [
        pl.BlockSpec(dw_w.shape, lambda b, i: (0, 0)),
        vec(BRANCH), vec(BRANCH), vec(BRANCH),
        pl.BlockSpec(w_out.shape, lambda b, i: (0, 0)),
        vec(D_MODEL), vec(D_MODEL),
    ]
    return pl.pallas_call(
        functools.partial(_cd_out_kernel, alpha=alpha),
        grid=(bsz, n // tm),
        in_specs=in_specs,
        out_specs=hspec,
        out_shape=jax.ShapeDtypeStruct((bsz, n, D_MODEL), F32),
        scratch_shapes=[pltpu.VMEM((tm + 2 * CONV_HALO, BRANCH), F32), pltpu.VMEM((tm, BRANCH), F32)],
        compiler_params=_params(2),
        name="cd_out",
    )(glu, glu, glu, cg, dm, h, mod, dw_w, dw_b, cng, cnb, w_out, lg, lb)


def kernel(x, c, ctx, c_ctx, mod_w, mod_b, ln_g, ln_b, ab_w_in, ab_w_out, a_w_s, a_b_s, a_norm_g, a_norm_b,
           b_lq1, b_lk1, b_lq2, b_lk2, b_subln_g, cd_w_in, cd_w_out, c_dw_w, c_dw_b, c_norm_g, c_norm_b, d_sink):
    bsz, n, _ = x.shape
    assert DEPTH == 2 and bsz + 1 <= MOD_ROWS
    alpha = (2.0 * DEPTH) ** 0.25
    tables = _rope_tables(n)

    cc = jnp.concatenate([c, c_ctx[None, :], jnp.zeros((MOD_ROWS - bsz - 1, D_MODEL), F32)], axis=0)
    mod = _modulation(cc, mod_w, mod_b).reshape(DEPTH * MOD_ROWS, 1, 3 * D_MODEL)
    lat_row = lambda layer: (lambda b: layer * MOD_ROWS + b)
    ctx_row = lambda layer: (lambda b: layer * MOD_ROWS + bsz)
    row2d = lambda a: a.reshape(1, -1)

    w_in0 = ab_w_in[0].astype(BF16)
    w_out0 = ab_w_out[0].astype(BF16)
    ws = a_w_s[0].astype(BF16)
    bs2d = jnp.repeat(a_b_s[0].T, LANES, axis=1)
    ng, nbias = row2d(a_norm_g[0]), row2d(a_norm_b[0])
    lvec = jnp.stack([b_lq1[0], b_lk1[0], b_lq2[0], b_lk2[0]])
    subln = row2d(b_subln_g[0])
    lam_init0 = 0.8 - 0.6 * math.exp(-0.3 * 0)
    lg0, lb0 = row2d(ln_g[0]), row2d(ln_b[0])

    ta, vn, q, k, v, sg = _ab_in(x, mod, lat_row(0), w_in0, ng, nbias, tables, tm=512)
    cta, cvn, cq, ck, cv, csg = _ab_in(ctx, mod, ctx_row(0), w_in0, ng, nbias, None, tm=CTX_LEN)
    bm = _diff_attn(lvec, subln, q, sg, [(ck, cv), (k, v)], lam_init0, tq=256)
    cbm = _diff_attn(lvec, subln, cq, csg, [(ck, cv)], lam_init0, tq=CTX_LEN)
    h1 = _ab_out(ta, vn, bm, x, mod, lat_row(0), ws, bs2d, w_out0, lg0, lb0, alpha, tm=512)
    hc1 = _ab_out(cta, cvn, cbm, ctx, mod, ctx_row(0), ws, bs2d, w_out0, lg0, lb0, alpha, tm=CTX_LEN)

    w_in1 = cd_w_in[0].astype(BF16)
    w_out1 = cd_w_out[0].astype(BF16)
    kv_lo = 4 * BRANCH
    glu, cg, dq, dk, dks, dv, dvs, dg = _cd_in(h1, mod, lat_row(1), w_in1, tables, tm=512)
    ck1, cks1, cv1, cvs1 = _cd_ctx_kv(hc1, mod, ctx_row(1), w_in1[:, kv_lo:kv_lo + 2 * LANES])
    dm = _swa(d_sink[0], dq, dg, dk, dks, dv, dvs, ck1, cks1, cv1, cvs1)
    return _cd_out(glu, cg, dm, h1, mod, lat_row(1), c_dw_w[0], row2d(c_dw_b[0]), row2d(c_norm_g[0]),
                   row2d(c_norm_b[0]), w_out1, row2d(ln_g[1]), row2d(ln_b[1]), alpha, tm=512)
```

```python
import functools
import math

import jax
import jax.numpy as jnp
import numpy as np
from jax import lax
from jax.experimental import pallas as pl
from jax.experimental.pallas import tpu as pltpu

F32 = jnp.float32
BF16 = jnp.bfloat16

D_MODEL = 1024
DEPTH = 2
CTX_LEN = 256
GRID_W = 64
ROPE_THETA = 10000.0
LN_EPS = 1e-6
RMS_EPS = 1e-5
NEG_INF = -1e30
HEAD_DIM = 64
BRANCH = 512
CHUNK = 128
C_KERNEL = 31
CONV_HALO = 16
MOD_ROWS = 24
LANES = 128

VMEM_LIMIT = 56 * 1024 * 1024
LOG2E = math.log2(math.e)
Q_SCALE = HEAD_DIM ** -0.5 * LOG2E
KEY_CHUNK = 256
CONV_ROWS = 128
ROW_TILE = 1024
ATTN_TILE = 256

_NT = (((1,), (1,)), ((), ()))


def _params(n_axes, vmem=VMEM_LIMIT):
    return pltpu.CompilerParams(dimension_semantics=("arbitrary",) * n_axes, vmem_limit_bytes=vmem)


def _layer_norm(x, g, b):
    mu = jnp.mean(x, axis=-1, keepdims=True)
    xc = x - mu
    var = jnp.mean(xc * xc, axis=-1, keepdims=True)
    return xc * lax.rsqrt(var + LN_EPS) * g + b


def _lane_mask(shape, lo):
    lane = lax.broadcasted_iota(jnp.int32, shape, len(shape) - 1) % LANES
    return (lane < HEAD_DIM) if lo else (lane >= HEAD_DIM)


def _mod_kernel(c_ref, w_ref, b_ref, o_ref):
    c = c_ref[...]
    a = c * jax.nn.sigmoid(c)
    a_hi = a.astype(BF16)
    a_lo = (a - a_hi.astype(F32)).astype(BF16)
    w = w_ref[0]
    w_hi = w.astype(BF16)
    w_lo = (w - w_hi.astype(F32)).astype(BF16)
    acc = jnp.dot(a_hi, w_hi, preferred_element_type=F32)
    acc = acc + jnp.dot(a_hi, w_lo, preferred_element_type=F32)
    acc = acc + jnp.dot(a_lo, w_hi, preferred_element_type=F32)
    o_ref[0] = acc + b_ref[0]


def _modulation(cc, mod_w, mod_b):
    tn = 768
    n_out = 3 * D_MODEL
    return pl.pallas_call(
        _mod_kernel,
        grid=(DEPTH, n_out // tn),
        in_specs=[
            pl.BlockSpec((MOD_ROWS, D_MODEL), lambda l, j: (0, 0)),
            pl.BlockSpec((1, D_MODEL, tn), lambda l, j: (l, 0, j)),
            pl.BlockSpec((1, 1, tn), lambda l, j: (l, 0, j)),
        ],
        out_specs=pl.BlockSpec((1, MOD_ROWS, tn), lambda l, j: (l, 0, j)),
        out_shape=jax.ShapeDtypeStruct((DEPTH, MOD_ROWS, n_out), F32),
        compiler_params=_params(2),
        name="modulation",
    )(cc, mod_w, mod_b.reshape(DEPTH, 1, n_out))


def _mod_specs(row_fn, parts):
    return [pl.BlockSpec((1, 1, D_MODEL), functools.partial(lambda b, i, p: (row_fn(b), 0, p), p=p))
            for p in parts]


def _rope_tables(n):
    m = HEAD_DIM // 4
    inv = ROPE_THETA ** (-np.arange(m, dtype=np.float64) / m)
    t = np.arange(n)
    ang_r = (t // GRID_W)[:, None] * inv
    ang_c = (t % GRID_W)[:, None] * inv
    z = np.zeros_like(ang_r)
    cos = np.concatenate([np.cos(ang_r)] * 2 + [np.cos(ang_c)] * 2, axis=-1)
    sin_up = np.concatenate([-np.sin(ang_r), z, -np.sin(ang_c), z], axis=-1)
    sin_dn = np.concatenate([z, np.sin(ang_r), z, np.sin(ang_c)], axis=-1)
    tile = lambda a: jnp.asarray(np.concatenate([a, a], axis=-1), dtype=F32)
    return tile(cos), tile(sin_up), tile(sin_dn)


def _rope(x, cos, sin_up, sin_dn):
    outs = []
    for s in range(x.shape[-1] // LANES):
        xs = x[:, s * LANES:(s + 1) * LANES]
        up = pltpu.roll(xs, LANES - 16, 1)
        dn = pltpu.roll(xs, 16, 1)
        outs.append(xs * cos + up * sin_up + dn * sin_dn)
    return outs[0] if len(outs) == 1 else jnp.concatenate(outs, axis=-1)


def _silu(x):
    return x * jax.nn.sigmoid(x)


def _ab_in_kernel(*refs, rope):
    if rope:
        (x_ref, shift_ref, scale_ref, w_ref, ng_ref, nb_ref, cos_ref, su_ref, sd_ref,
         ta_ref, vn_ref, q_ref, k_ref, v_ref, sg_ref) = refs
    else:
        (x_ref, shift_ref, scale_ref, w_ref, ng_ref, nb_ref,
         ta_ref, vn_ref, q_ref, k_ref, v_ref, sg_ref) = refs
    u = (x_ref[0] * (1.0 + scale_ref[0]) + shift_ref[0]).astype(BF16)

    def seg(s):
        return jnp.dot(u, w_ref[:, s * BRANCH:(s + 1) * BRANCH], preferred_element_type=F32)

    ta_ref[0] = (jax.nn.gelu(seg(0)) * _silu(seg(2))).astype(BF16)
    vn_ref[0] = _layer_norm(jax.nn.gelu(seg(1)), ng_ref[...], nb_ref[...]).astype(BF16)
    q = seg(3)
    k = seg(4)
    if rope:
        tabs = (cos_ref[...], su_ref[...], sd_ref[...])
        q = _rope(q, *tabs)
        k = _rope(k, *tabs)
    q_ref[0] = (q * Q_SCALE).astype(BF16)
    k_ref[0] = k.astype(BF16)
    v_ref[0] = seg(5).astype(BF16)
    sg_ref[0] = _silu(seg(6)).astype(BF16)


def _ab_in(h, mod, mod_row, w_in, ng, nb, tables, tm):
    bsz, n, _ = h.shape
    rope = tables is not None
    in_specs = [pl.BlockSpec((1, tm, D_MODEL), lambda b, i: (b, i, 0))]
    in_specs += _mod_specs(mod_row, (0, 1))
    in_specs += [
        pl.BlockSpec(w_in.shape, lambda b, i: (0, 0)),
        pl.BlockSpec((1, BRANCH), lambda b, i: (0, 0)),
        pl.BlockSpec((1, BRANCH), lambda b, i: (0, 0)),
    ]
    args = [h, mod, mod, w_in, ng, nb]
    if rope:
        in_specs += [pl.BlockSpec((tm, LANES), lambda b, i: (i, 0))] * 3
        args += list(tables)
    out_spec = pl.BlockSpec((1, tm, BRANCH), lambda b, i: (b, i, 0))
    out_shape = jax.ShapeDtypeStruct((bsz, n, BRANCH), BF16)
    return pl.pallas_call(
        functools.partial(_ab_in_kernel, rope=rope),
        grid=(bsz, n // tm),
        in_specs=in_specs,
        out_specs=[out_spec] * 6,
        out_shape=[out_shape] * 6,
        compiler_params=_params(2),
        name="ab_in_rope" if rope else "ab_in_ctx",
    )(*args)


def _diff_attn_kernel(*refs, seg_rows, lam_init):
    n_seg = len(seg_rows)
    lv_ref, g_ref, q_ref, sg_ref = refs[:4]
    kv = refs[4:4 + 2 * n_seg]
    o_ref = refs[4 + 2 * n_seg]
    vext_ref = refs[5 + 2 * n_seg]
    heads = BRANCH // LANES

    @pl.when(pl.program_id(1) == 0)
    def _fill_values():
        for h in range(heads):
            base = 0
            for j, rows in enumerate(seg_rows):
                vext_ref[h, base:base + rows, :LANES] = kv[2 * j + 1][0, :, h * LANES:(h + 1) * LANES]
                base += rows
            vext_ref[h, :, LANES:] = jnp.ones((base, LANES), BF16)

    lv = lv_ref[...]
    lam = (jnp.exp(jnp.sum(lv[0:1] * lv[1:2], axis=-1, keepdims=True))
           - jnp.exp(jnp.sum(lv[2:3] * lv[3:4], axis=-1, keepdims=True)) + lam_init)
    tq = q_ref.shape[1]
    lo = _lane_mask((tq, LANES), True)
    for h in range(heads):
        cols = slice(h * LANES, (h + 1) * LANES)
        q = q_ref[0, :, cols]
        zero = jnp.zeros_like(q)
        qs = jnp.concatenate([jnp.where(lo, q, zero), jnp.where(lo, zero, q)], axis=0)
        m = acc = None
        base = 0
        for j, rows in enumerate(seg_rows):
            for c0 in range(0, rows, KEY_CHUNK):
                n = min(KEY_CHUNK, rows - c0)
                s = lax.dot_general(qs, kv[2 * j][0, c0:c0 + n, cols], _NT, preferred_element_type=F32)
                smax = jnp.max(s, axis=-1, keepdims=True)
                m_new = smax if m is None else jnp.maximum(m, smax)
                p = jnp.exp2(s - m_new).astype(BF16)
                pv = jnp.dot(p, vext_ref[h, base + c0:base + c0 + n, :], preferred_element_type=F32)
                acc = pv if acc is None else jnp.exp2(m - m_new) * acc + pv
                m = m_new
            base += rows
        on = acc[:, :LANES] * (1.0 / acc[:, LANES:])
        o = on[:tq] - lam * on[tq:]
        o = o * lax.rsqrt(jnp.mean(o * o, axis=-1, keepdims=True) + RMS_EPS) * g_ref[...] * (1.0 - lam_init)
        o_ref[0, :, cols] = (o * sg_ref[0, :, cols].astype(F32)).astype(BF16)


def _diff_attn(lvec, subln_g, q, sg, segs, lam_init, tq):
    bsz, n, _ = q.shape
    qspec = pl.BlockSpec((1, tq, BRANCH), lambda b, i: (b, i, 0))
    in_specs = [
        pl.BlockSpec(lvec.shape, lambda b, i: (0, 0)),
        pl.BlockSpec((1, LANES), lambda b, i: (0, 0)),
        qspec, qspec,
    ]
    args = [lvec, subln_g, q, sg]
    for k, v in segs:
        spec = pl.BlockSpec((1, k.shape[1], BRANCH), lambda b, i: (b, 0, 0))
        in_specs += [spec, spec]
        args += [k, v]
    seg_rows = tuple(k.shape[1] for k, _ in segs)
    return pl.pallas_call(
        functools.partial(_diff_attn_kernel, seg_rows=seg_rows, lam_init=lam_init),
        grid=(bsz, n // tq),
        in_specs=in_specs,
        out_specs=qspec,
        out_shape=jax.ShapeDtypeStruct((bsz, n, BRANCH), BF16),
        scratch_shapes=[pltpu.VMEM((BRANCH // LANES, sum(seg_rows), 2 * LANES), BF16)],
        compiler_params=_params(2),
        name=f"diff_attn_{len(segs)}seg",
    )(*args)


def _out_proj_ln(left, right, w_ref, h, gate, lg, lb, alpha):
    y = jnp.dot(left, w_ref[:BRANCH, :], preferred_element_type=F32)
    y = y + jnp.dot(right, w_ref[BRANCH:, :], preferred_element_type=F32)
    return _layer_norm(alpha * h + gate * y, lg, lb)


def _ab_out_kernel(ta_ref, vn_ref, bm_ref, h_ref, gate_ref, ws_ref, bs_ref, w_ref, lg_ref, lb_ref, o_ref,
                   *, alpha):
    tm = ta_ref.shape[1]
    groups = BRANCH // LANES
    rows = []
    for c in range(tm // CHUNK):
        cols = []
        for g in range(groups):
            vc = vn_ref[0, c * CHUNK:(c + 1) * CHUNK, g * LANES:(g + 1) * LANES]
            cols.append(jnp.dot(ws_ref[g], vc, preferred_element_type=F32))
        rows.append(jnp.concatenate(cols, axis=-1) + bs_ref[...])
    mixed = rows[0] if len(rows) == 1 else jnp.concatenate(rows, axis=0)
    a_mix = (ta_ref[0].astype(F32) * mixed).astype(BF16)
    o_ref[0] = _out_proj_ln(a_mix, bm_ref[0], w_ref, h_ref[0], gate_ref[0], lg_ref[...], lb_ref[...], alpha)


def _ab_out(ta, vn, bm, h, mod, mod_row, ws, bs2d, w_out, lg, lb, alpha, tm):
    bsz, n, _ = h.shape
    bspec = pl.BlockSpec((1, tm, BRANCH), lambda b, i: (b, i, 0))
    hspec = pl.BlockSpec((1, tm, D_MODEL), lambda b, i: (b, i, 0))
    in_specs = [bspec, bspec, bspec, hspec] + _mod_specs(mod_row, (2,)) + [
        pl.BlockSpec(ws.shape, lambda b, i: (0, 0, 0)),
        pl.BlockSpec(bs2d.shape, lambda b, i: (0, 0)),
        pl.BlockSpec(w_out.shape, lambda b, i: (0, 0)),
        pl.BlockSpec((1, D_MODEL), lambda b, i: (0, 0)),
        pl.BlockSpec((1, D_MODEL), lambda b, i: (0, 0)),
    ]
    return pl.pallas_call(
        functools.partial(_ab_out_kernel, alpha=alpha),
        grid=(bsz, n // tm),
        in_specs=in_specs,
        out_specs=hspec,
        out_shape=jax.ShapeDtypeStruct((bsz, n, D_MODEL), F32),
        compiler_params=_params(2),
        name="ab_out",
    )(ta, vn, bm, h, mod, ws, bs2d, w_out, lg, lb)


def _cd_in_kernel(x_ref, shift_ref, scale_ref, w_ref, cos_ref, su_ref, sd_ref,
                  glu_ref, cg_ref, q_ref, k_ref, ks_ref, v_ref, vs_ref, dg_ref):
    u = (x_ref[0] * (1.0 + scale_ref[0]) + shift_ref[0]).astype(BF16)

    def cols(lo, width):
        return jnp.dot(u, w_ref[:, lo:lo + width], preferred_element_type=F32)

    glu_ref[0] = cols(0, BRANCH) * jax.nn.sigmoid(cols(BRANCH, BRANCH))
    cg_ref[0] = _silu(cols(2 * BRANCH, BRANCH)).astype(BF16)
    tabs = (cos_ref[...], su_ref[...], sd_ref[...])
    q_ref[0] = (_rope(cols(3 * BRANCH, BRANCH), *tabs) * Q_SCALE).astype(BF16)
    k = _rope(cols(4 * BRANCH, LANES), *tabs)
    v = cols(4 * BRANCH + LANES, LANES)
    k_ref[0] = k.astype(BF16)
    ks_ref[0] = pltpu.roll(k, HEAD_DIM, 1).astype(BF16)
    v_ref[0] = v.astype(BF16)
    vs_ref[0] = pltpu.roll(v, HEAD_DIM, 1).astype(BF16)
    dg_ref[0] = _silu(cols(4 * BRANCH + 2 * LANES, BRANCH)).astype(BF16)


def _cd_in(h, mod, mod_row, w_in, tables, tm):
    bsz, n, _ = h.shape
    in_specs = [pl.BlockSpec((1, tm, D_MODEL), lambda b, i: (b, i, 0))] + _mod_specs(mod_row, (0, 1))
    in_specs += [pl.BlockSpec(w_in.shape, lambda b, i: (0, 0))]
    in_specs += [pl.BlockSpec((tm, LANES), lambda b, i: (i, 0))] * 3
    wide = pl.BlockSpec((1, tm, BRANCH), lambda b, i: (b, i, 0))
    narrow = pl.BlockSpec((1, tm, LANES), lambda b, i: (b, i, 0))
    wide_bf = jax.ShapeDtypeStruct((bsz, n, BRANCH), BF16)
    narrow_bf = jax.ShapeDtypeStruct((bsz, n, LANES), BF16)
    return pl.pallas_call(
        _cd_in_kernel,
        grid=(bsz, n // tm),
        in_specs=in_specs,
        out_specs=[wide, wide, wide, narrow, narrow, narrow, narrow, wide],
        out_shape=[jax.ShapeDtypeStruct((bsz, n, BRANCH), F32), wide_bf, wide_bf,
                   narrow_bf, narrow_bf, narrow_bf, narrow_bf, wide_bf],
        compiler_params=_params(2),
        name="cd_in",
    )(h, mod, mod, w_in, *tables)


def _cd_ctx_kv_kernel(x_ref, shift_ref, scale_ref, w_ref, k_ref, ks_ref, v_ref, vs_ref):
    u = (x_ref[0] * (1.0 + scale_ref[0]) + shift_ref[0]).astype(BF16)
    kv = jnp.dot(u, w_ref[...], preferred_element_type=F32)
    k = kv[:, :LANES]
    v = kv[:, LANES:]
    k_ref[0] = k.astype(BF16)
    ks_ref[0] = pltpu.roll(k, HEAD_DIM, 1).astype(BF16)
    v_ref[0] = v.astype(BF16)
    vs_ref[0] = pltpu.roll(v, HEAD_DIM, 1).astype(BF16)


def _cd_ctx_kv(h, mod, mod_row, w_kv):
    bsz, n, _ = h.shape
    in_specs = [pl.BlockSpec((1, n, D_MODEL), lambda b: (b, 0, 0))]
    in_specs += [pl.BlockSpec((1, 1, D_MODEL), functools.partial(lambda b, p: (mod_row(b), 0, p), p=p))
                 for p in (0, 1)]
    in_specs += [pl.BlockSpec(w_kv.shape, lambda b: (0, 0))]
    spec = pl.BlockSpec((1, n, LANES), lambda b: (b, 0, 0))
    shape = jax.ShapeDtypeStruct((bsz, n, LANES), BF16)
    return pl.pallas_call(
        _cd_ctx_kv_kernel,
        grid=(bsz,),
        in_specs=in_specs,
        out_specs=[spec] * 4,
        out_shape=[shape] * 4,
        compiler_params=_params(1),
        name="cd_ctx_kv",
    )(h, mod, mod, w_kv)


def _swa_kernel(sink_ref, q_ref, dg_ref,
                kp_ref, kc_ref, kn_ref, ksp_ref, ksc_ref, ksn_ref,
                vp_ref, vc_ref, vn_ref, vsp_ref, vsc_ref, vsn_ref,
                ck_ref, cks_ref, cv_ref, cvs_ref, o_ref):
    i = pl.program_id(1)
    nsteps = pl.num_programs(1)
    n_slab = BRANCH // LANES
    stack = lambda a: jnp.concatenate([a] * n_slab, axis=0)
    r = lax.broadcasted_iota(jnp.int32, (CHUNK, CHUNK), 0)
    c = lax.broadcasted_iota(jnp.int32, (CHUNK, CHUNK), 1)
    tri_prev = stack(c >= r)
    tri_next = stack(c <= r)
    edge_prev = stack(c >= r + jnp.where(i > 0, 0, CHUNK))
    edge_next = stack(c <= r - jnp.where(i < nsteps - 1, 0, CHUNK))
    lo = _lane_mask((CHUNK, LANES), True)
    row = lax.broadcasted_iota(jnp.int32, (n_slab * CHUNK, 1), 0)
    first, second = slice(0, CHUNK), slice(CHUNK, 2 * CHUNK)

    layouts = (
        ((0, 2, 5, 7), (kp_ref, kc_ref, kn_ref, ck_ref), (vp_ref, vc_ref, vn_ref, cv_ref)),
        ((1, 3, 4, 6), (ksp_ref, ksc_ref, ksn_ref, cks_ref), (vsp_ref, vsc_ref, vsn_ref, cvs_ref)),
    )
    for rows in (first, second):
        slabs = [q_ref[0, rows, j * LANES:(j + 1) * LANES] for j in range(n_slab)]
        zero = jnp.zeros_like(slabs[0])
        q_lo = [jnp.where(lo, q, zero) for q in slabs]
        q_hi = [jnp.where(lo, zero, q) for q in slabs]
        normed = []
        for swapped, (heads, (kp, kc, kn, kx), (vp, vc, vn, vx)) in enumerate(layouts):
            qs = jnp.concatenate((q_hi[:2] + q_lo[2:]) if swapped else (q_lo[:2] + q_hi[2:]), axis=0)
            if rows is first:
                keys = (kp[0], kc[0, first], kc[0, second], kx[0])
                vals = (vp[0], vc[0, first], vc[0, second], vx[0])
                masks = (edge_prev, None, tri_next, None)
            else:
                keys = (kc[0, first], kc[0, second], kn[0], kx[0])
                vals = (vc[0, first], vc[0, second], vn[0], vx[0])
                masks = (tri_prev, None, edge_next, None)
            sink = jnp.zeros((n_slab * CHUNK, 1), F32)
            for j, head in enumerate(heads):
                sink = jnp.where((row >= j * CHUNK) & (row < (j + 1) * CHUNK), sink_ref[head] * LOG2E, sink)
            ss = [lax.dot_general(qs, kb, _NT, preferred_element_type=F32) for kb in keys]
            ss = [s if mk is None else jnp.where(mk, s, NEG_INF) for s, mk in zip(ss, masks)]
            split = lambda a: [a[:, c0:c0 + LANES] for c0 in range(0, a.shape[1], LANES)]
            m = jnp.max(functools.reduce(jnp.maximum, sum((split(s) for s in ss), [])), axis=-1, keepdims=True)
            m = jnp.maximum(m, sink)
            ps = [jnp.exp2(s - m) for s in ss]
            l = jnp.sum(functools.reduce(jnp.add, sum((split(p) for p in ps), [])), axis=-1, keepdims=True)
            l = l + jnp.exp2(sink - m)
            acc = None
            for p, vb in zip(ps, vals):
                pv = jnp.dot(p.astype(BF16), vb, preferred_element_type=F32)
                acc = pv if acc is None else acc + pv
            normed.append(acc * (1.0 / l))
        plain, swapped = normed
        out = []
        for j in range(n_slab):
            a = plain[j * CHUNK:(j + 1) * CHUNK]
            b = swapped[j * CHUNK:(j + 1) * CHUNK]
            out.append(jnp.where(lo, a, b) if j < 2 else jnp.where(lo, b, a))
        o = jnp.concatenate(out, axis=-1)
        o_ref[0, rows, :] = (o * dg_ref[0, rows, :].astype(F32)).astype(BF16)


def _swa(sink, q, dg, k, ks, v, vs, ck, cks, cv, cvs):
    bsz, n, _ = q.shape
    nb = n // CHUNK
    qspec = pl.BlockSpec((1, 2 * CHUNK, BRANCH), lambda b, i: (b, i, 0))
    prev = pl.BlockSpec((1, CHUNK, LANES), lambda b, i: (b, jnp.maximum(2 * i - 1, 0), 0))
    cur = pl.BlockSpec((1, 2 * CHUNK, LANES), lambda b, i: (b, i, 0))
    nxt = pl.BlockSpec((1, CHUNK, LANES), lambda b, i: (b, jnp.minimum(2 * i + 2, nb - 1), 0))
    cspec = pl.BlockSpec((1, ck.shape[1], LANES), lambda b, i: (b, 0, 0))
    in_specs = [pl.BlockSpec(memory_space=pltpu.SMEM), qspec, qspec] + [prev, cur, nxt] * 4 + [cspec] * 4
    return pl.pallas_call(
        _swa_kernel,
        grid=(bsz, nb // 2),
        in_specs=in_specs,
        out_specs=qspec,
        out_shape=jax.ShapeDtypeStruct((bsz, n, BRANCH), BF16),
        compiler_params=_params(2),
        name="swa",
    )(sink, q, dg, k, k, k, ks, ks, ks, v, v, v, vs, vs, vs, ck, cks, cv, cvs)


def _cd_out_kernel(glu_ref, gp_ref, gn_ref, cg_ref, dm_ref, h_ref, gate_ref, dw_ref, db_ref, cng_ref, cnb_ref,
                   w_ref, lg_ref, lb_ref, o_ref, ext_ref, y_ref, *, alpha):
    i = pl.program_id(1)
    nb = pl.num_programs(1)
    tm = glu_ref.shape[1]
    ext_ref[0:CONV_HALO, :] = jnp.where(i > 0, gp_ref[0], 0.0)
    ext_ref[CONV_HALO:CONV_HALO + tm, :] = glu_ref[0]
    ext_ref[CONV_HALO + tm:, :] = jnp.where(i < nb - 1, gn_ref[0], 0.0)
    off = CONV_HALO - C_KERNEL // 2
    span = CONV_ROWS + 8

    def conv_rows(t, carry):
        r0 = pl.multiple_of(t * CONV_ROWS, CONV_ROWS)
        for c0 in range(0, BRANCH, LANES):
            win = ext_ref[pl.ds(r0, CONV_ROWS + 2 * CONV_HALO), c0:c0 + LANES]
            y = None
            for s in range(8):
                z = None
                for a in range((off + C_KERNEL + 7) // 8):
                    j = 8 * a + s - off
                    if 0 <= j < C_KERNEL:
                        term = win[8 * a:8 * a + span] * dw_ref[j:j + 1, c0:c0 + LANES]
                        z = term if z is None else z + term
                zs = z[s:s + CONV_ROWS]
                y = zs if y is None else y + zs
            y_ref[pl.ds(r0, CONV_ROWS), c0:c0 + LANES] = y
        return carry

    lax.fori_loop(0, tm // CONV_ROWS, conv_rows, 0)
    y = y_ref[...] + db_ref[...]
    c_lat = _silu(_layer_norm(y, cng_ref[...], cnb_ref[...]))
    c_mix = (c_lat * cg_ref[0].astype(F32)).astype(BF16)
    o_ref[0] = _out_proj_ln(c_mix, dm_ref[0], w_ref, h_ref[0], gate_ref[0], lg_ref[...], lb_ref[...], alpha)


def _cd_out(glu, cg, dm, h, mod, mod_row, dw_w, dw_b, cng, cnb, w_out, lg, lb, alpha, tm):
    bsz, n, _ = h.shape
    per = tm // CONV_HALO
    last = n // CONV_HALO - 1
    bspec = pl.BlockSpec((1, tm, BRANCH), lambda b, i: (b, i, 0))
    hspec = pl.BlockSpec((1, tm, D_MODEL), lambda b, i: (b, i, 0))
    vec = lambda width: pl.BlockSpec((1, width), lambda b, i: (0, 0))
    in_specs = [
        bspec,
        pl.BlockSpec((1, CONV_HALO, BRANCH), lambda b, i: (b, jnp.maximum(i * per - 1, 0), 0)),
        pl.BlockSpec((1, CONV_HALO, BRANCH), lambda b, i: (b, jnp.minimum((i + 1) * per, last), 0)),
        bspec, bspec, hspec,
    ] + _mod_specs(mod_row, (2,)) + [
        pl.BlockSpec(dw_w.shape, lambda b, i: (0, 0)),
        vec(BRANCH), vec(BRANCH), vec(BRANCH),
        pl.BlockSpec(w_out.shape, lambda b, i: (0, 0)),
        vec(D_MODEL), vec(D_MODEL),
    ]
    return pl.pallas_call(
        functools.partial(_cd_out_kernel, alpha=alpha),
        grid=(bsz, n // tm),
        in_specs=in_specs,
        out_specs=hspec,
        out_shape=jax.ShapeDtypeStruct((bsz, n, D_MODEL), F32),
        scratch_shapes=[pltpu.VMEM((tm + 2 * CONV_HALO, BRANCH), F32), pltpu.VMEM((tm, BRANCH), F32)],
        compiler_params=_params(2),
        name="cd_out",
    )(glu, glu, glu, cg, dm, h, mod, dw_w, dw_b, cng, cnb, w_out, lg, lb)


def kernel(x, c, ctx, c_ctx, mod_w, mod_b, ln_g, ln_b, ab_w_in, ab_w_out, a_w_s, a_b_s, a_norm_g, a_norm_b,
           b_lq1, b_lk1, b_lq2, b_lk2, b_subln_g, cd_w_in, cd_w_out, c_dw_w, c_dw_b, c_norm_g, c_norm_b, d_sink):
    bsz, n, _ = x.shape
    assert DEPTH == 2 and bsz + 1 <= MOD_ROWS
    alpha = (2.0 * DEPTH) ** 0.25
    tables = _rope_tables(n)

    cc = jnp.concatenate([c, c_ctx[None, :], jnp.zeros((MOD_ROWS - bsz - 1, D_MODEL), F32)], axis=0)
    mod = _modulation(cc, mod_w, mod_b).reshape(DEPTH * MOD_ROWS, 1, 3 * D_MODEL)
    lat_row = lambda layer: (lambda b: layer * MOD_ROWS + b)
    ctx_row = lambda layer: (lambda b: layer * MOD_ROWS + bsz)
    row2d = lambda a: a.reshape(1, -1)

    w_in0 = ab_w_in[0].astype(BF16)
    w_out0 = ab_w_out[0].astype(BF16)
    ws = a_w_s[0].astype(BF16)
    bs2d = jnp.repeat(a_b_s[0].T, LANES, axis=1)
    ng, nbias = row2d(a_norm_g[0]), row2d(a_norm_b[0])
    lvec = jnp.stack([b_lq1[0], b_lk1[0], b_lq2[0], b_lk2[0]])
    subln = row2d(b_subln_g[0])
    lam_init0 = 0.8 - 0.6 * math.exp(-0.3 * 0)
    lg0, lb0 = row2d(ln_g[0]), row2d(ln_b[0])

    ta, vn, q, k, v, sg = _ab_in(x, mod, lat_row(0), w_in0, ng, nbias, tables, tm=ROW_TILE)
    cta, cvn, cq, ck, cv, csg = _ab_in(ctx, mod, ctx_row(0), w_in0, ng, nbias, None, tm=CTX_LEN)
    bm = _diff_attn(lvec, subln, q, sg, [(ck, cv), (k, v)], lam_init0, tq=ATTN_TILE)
    cbm = _diff_attn(lvec, subln, cq, csg, [(ck, cv)], lam_init0, tq=CTX_LEN)
    h1 = _ab_out(ta, vn, bm, x, mod, lat_row(0), ws, bs2d, w_out0, lg0, lb0, alpha, tm=ROW_TILE)
    hc1 = _ab_out(cta, cvn, cbm, ctx, mod, ctx_row(0), ws, bs2d, w_out0, lg0, lb0, alpha, tm=CTX_LEN)

    w_in1 = cd_w_in[0].astype(BF16)
    w_out1 = cd_w_out[0].astype(BF16)
    kv_lo = 4 * BRANCH
    glu, cg, dq, dk, dks, dv, dvs, dg = _cd_in(h1, mod, lat_row(1), w_in1, tables, tm=ROW_TILE)
    ck1, cks1, cv1, cvs1 = _cd_ctx_kv(hc1, mod, ctx_row(1), w_in1[:, kv_lo:kv_lo + 2 * LANES])
    dm = _swa(d_sink[0], dq, dg, dk, dks, dv, dvs, ck1, cks1, cv1, cvs1)
    return _cd_out(glu, cg, dm, h1, mod, lat_row(1), c_dw_w[0], row2d(c_dw_b[0]), row2d(c_norm_g[0]),
                   row2d(c_norm_b[0]), w_out1, row2d(ln_g[1]), row2d(ln_b[1]), alpha, tm=ROW_TILE)
```

```python
import functools
import math

import jax
import jax.numpy as jnp
import numpy as np
from jax import lax
from jax.experimental import pallas as pl
from jax.experimental.pallas import tpu as pltpu

F32 = jnp.float32
BF16 = jnp.bfloat16

D_MODEL = 1024
DEPTH = 2
CTX_LEN = 256
GRID_W = 64
ROPE_THETA = 10000.0
LN_EPS = 1e-6
RMS_EPS = 1e-5
NEG_INF = -1e30
HEAD_DIM = 64
BRANCH = 512
CHUNK = 128
C_KERNEL = 31
CONV_HALO = 16
MOD_ROWS = 24
LANES = 128

VMEM_LIMIT = 56 * 1024 * 1024
LOG2E = math.log2(math.e)
Q_SCALE = HEAD_DIM ** -0.5 * LOG2E
KEY_CHUNK = 256
CONV_ROWS = 128
ROW_TILE = 1024
ATTN_TILE = 256
SWA_BLOCKS = 8

_NT = (((1,), (1,)), ((), ()))


def _params(n_axes, vmem=VMEM_LIMIT):
    return pltpu.CompilerParams(dimension_semantics=("arbitrary",) * n_axes, vmem_limit_bytes=vmem)


def _layer_norm(x, g, b):
    mu = jnp.mean(x, axis=-1, keepdims=True)
    xc = x - mu
    var = jnp.mean(xc * xc, axis=-1, keepdims=True)
    return xc * lax.rsqrt(var + LN_EPS) * g + b


def _lane_mask(shape, lo):
    lane = lax.broadcasted_iota(jnp.int32, shape, len(shape) - 1) % LANES
    return (lane < HEAD_DIM) if lo else (lane >= HEAD_DIM)


def _mod_kernel(c_ref, w_ref, b_ref, o_ref):
    c = c_ref[...]
    a = c * jax.nn.sigmoid(c)
    a_hi = a.astype(BF16)
    a_lo = (a - a_hi.astype(F32)).astype(BF16)
    w = w_ref[0]
    w_hi = w.astype(BF16)
    w_lo = (w - w_hi.astype(F32)).astype(BF16)
    acc = jnp.dot(a_hi, w_hi, preferred_element_type=F32)
    acc = acc + jnp.dot(a_hi, w_lo, preferred_element_type=F32)
    acc = acc + jnp.dot(a_lo, w_hi, preferred_element_type=F32)
    o_ref[0] = acc + b_ref[0]


def _modulation(cc, mod_w, mod_b):
    tn = 768
    n_out = 3 * D_MODEL
    return pl.pallas_call(
        _mod_kernel,
        grid=(DEPTH, n_out // tn),
        in_specs=[
            pl.BlockSpec((MOD_ROWS, D_MODEL), lambda l, j: (0, 0)),
            pl.BlockSpec((1, D_MODEL, tn), lambda l, j: (l, 0, j)),
            pl.BlockSpec((1, 1, tn), lambda l, j: (l, 0, j)),
        ],
        out_specs=pl.BlockSpec((1, MOD_ROWS, tn), lambda l, j: (l, 0, j)),
        out_shape=jax.ShapeDtypeStruct((DEPTH, MOD_ROWS, n_out), F32),
        compiler_params=_params(2),
        name="modulation",
    )(cc, mod_w, mod_b.reshape(DEPTH, 1, n_out))


def _mod_specs(row_fn, parts):
    return [pl.BlockSpec((1, 1, D_MODEL), functools.partial(lambda b, i, p: (row_fn(b), 0, p), p=p))
            for p in parts]


def _rope_tables(n):
    m = HEAD_DIM // 4
    inv = ROPE_THETA ** (-np.arange(m, dtype=np.float64) / m)
    t = np.arange(n)
    ang_r = (t // GRID_W)[:, None] * inv
    ang_c = (t % GRID_W)[:, None] * inv
    z = np.zeros_like(ang_r)
    cos = np.concatenate([np.cos(ang_r)] * 2 + [np.cos(ang_c)] * 2, axis=-1)
    sin_up = np.concatenate([-np.sin(ang_r), z, -np.sin(ang_c), z], axis=-1)
    sin_dn = np.concatenate([z, np.sin(ang_r), z, np.sin(ang_c)], axis=-1)
    tile = lambda a: jnp.asarray(np.concatenate([a, a], axis=-1), dtype=F32)
    return tile(cos), tile(sin_up), tile(sin_dn)


def _rope(x, cos, sin_up, sin_dn):
    outs = []
    for s in range(x.shape[-1] // LANES):
        xs = x[:, s * LANES:(s + 1) * LANES]
        up = pltpu.roll(xs, LANES - 16, 1)
        dn = pltpu.roll(xs, 16, 1)
        outs.append(xs * cos + up * sin_up + dn * sin_dn)
    return outs[0] if len(outs) == 1 else jnp.concatenate(outs, axis=-1)


def _silu(x):
    return x * jax.nn.sigmoid(x)


def _ab_in_kernel(*refs, rope):
    if rope:
        (x_ref, shift_ref, scale_ref, w_ref, ng_ref, nb_ref, cos_ref, su_ref, sd_ref,
         ta_ref, vn_ref, q_ref, k_ref, v_ref, sg_ref) = refs
    else:
        (x_ref, shift_ref, scale_ref, w_ref, ng_ref, nb_ref,
         ta_ref, vn_ref, q_ref, k_ref, v_ref, sg_ref) = refs
    u = (x_ref[0] * (1.0 + scale_ref[0]) + shift_ref[0]).astype(BF16)

    def seg(s):
        return jnp.dot(u, w_ref[:, s * BRANCH:(s + 1) * BRANCH], preferred_element_type=F32)

    ta_ref[0] = (jax.nn.gelu(seg(0)) * _silu(seg(2))).astype(BF16)
    vn_ref[0] = _layer_norm(jax.nn.gelu(seg(1)), ng_ref[...], nb_ref[...]).astype(BF16)
    q = seg(3)
    k = seg(4)
    if rope:
        tabs = (cos_ref[...], su_ref[...], sd_ref[...])
        q = _rope(q, *tabs)
        k = _rope(k, *tabs)
    q_ref[0] = (q * Q_SCALE).astype(BF16)
    k_ref[0] = k.astype(BF16)
    v_ref[0] = seg(5).astype(BF16)
    sg_ref[0] = _silu(seg(6)).astype(BF16)


def _ab_in(h, mod, mod_row, w_in, ng, nb, tables, tm):
    bsz, n, _ = h.shape
    rope = tables is not None
    in_specs = [pl.BlockSpec((1, tm, D_MODEL), lambda b, i: (b, i, 0))]
    in_specs += _mod_specs(mod_row, (0, 1))
    in_specs += [
        pl.BlockSpec(w_in.shape, lambda b, i: (0, 0)),
        pl.BlockSpec((1, BRANCH), lambda b, i: (0, 0)),
        pl.BlockSpec((1, BRANCH), lambda b, i: (0, 0)),
    ]
    args = [h, mod, mod, w_in, ng, nb]
    if rope:
        in_specs += [pl.BlockSpec((tm, LANES), lambda b, i: (i, 0))] * 3
        args += list(tables)
    out_spec = pl.BlockSpec((1, tm, BRANCH), lambda b, i: (b, i, 0))
    out_shape = jax.ShapeDtypeStruct((bsz, n, BRANCH), BF16)
    return pl.pallas_call(
        functools.partial(_ab_in_kernel, rope=rope),
        grid=(bsz, n // tm),
        in_specs=in_specs,
        out_specs=[out_spec] * 6,
        out_shape=[out_shape] * 6,
        compiler_params=_params(2),
        name="ab_in_rope" if rope else "ab_in_ctx",
    )(*args)


def _diff_attn_kernel(*refs, seg_rows, lam_init):
    n_seg = len(seg_rows)
    lv_ref, g_ref, q_ref, sg_ref = refs[:4]
    kv = refs[4:4 + 2 * n_seg]
    o_ref = refs[4 + 2 * n_seg]
    vext_ref = refs[5 + 2 * n_seg]
    heads = BRANCH // LANES

    @pl.when(pl.program_id(1) == 0)
    def _fill_values():
        for h in range(heads):
            base = 0
            for j, rows in enumerate(seg_rows):
                vext_ref[h, base:base + rows, :LANES] = kv[2 * j + 1][0, :, h * LANES:(h + 1) * LANES]
                base += rows
            vext_ref[h, :, LANES:] = jnp.ones((base, LANES), BF16)

    lv = lv_ref[...]
    lam = (jnp.exp(jnp.sum(lv[0:1] * lv[1:2], axis=-1, keepdims=True))
           - jnp.exp(jnp.sum(lv[2:3] * lv[3:4], axis=-1, keepdims=True)) + lam_init)
    tq = q_ref.shape[1]
    lo = _lane_mask((tq, LANES), True)
    for h in range(heads):
        cols = slice(h * LANES, (h + 1) * LANES)
        q = q_ref[0, :, cols]
        zero = jnp.zeros_like(q)
        qs = jnp.concatenate([jnp.where(lo, q, zero), jnp.where(lo, zero, q)], axis=0)
        m = acc = None
        base = 0
        for j, rows in enumerate(seg_rows):
            for c0 in range(0, rows, KEY_CHUNK):
                n = min(KEY_CHUNK, rows - c0)
                s = lax.dot_general(qs, kv[2 * j][0, c0:c0 + n, cols], _NT, preferred_element_type=F32)
                smax = jnp.max(s, axis=-1, keepdims=True)
                m_new = smax if m is None else jnp.maximum(m, smax)
                p = jnp.exp2(s - m_new).astype(BF16)
                pv = jnp.dot(p, vext_ref[h, base + c0:base + c0 + n, :], preferred_element_type=F32)
                acc = pv if acc is None else jnp.exp2(m - m_new) * acc + pv
                m = m_new
            base += rows
        on = acc[:, :LANES] * (1.0 / acc[:, LANES:])
        o = on[:tq] - lam * on[tq:]
        o = o * lax.rsqrt(jnp.mean(o * o, axis=-1, keepdims=True) + RMS_EPS) * g_ref[...] * (1.0 - lam_init)
        o_ref[0, :, cols] = (o * sg_ref[0, :, cols].astype(F32)).astype(BF16)


def _diff_attn(lvec, subln_g, q, sg, segs, lam_init, tq):
    bsz, n, _ = q.shape
    qspec = pl.BlockSpec((1, tq, BRANCH), lambda b, i: (b, i, 0))
    in_specs = [
        pl.BlockSpec(lvec.shape, lambda b, i: (0, 0)),
        pl.BlockSpec((1, LANES), lambda b, i: (0, 0)),
        qspec, qspec,
    ]
    args = [lvec, subln_g, q, sg]
    for k, v in segs:
        spec = pl.BlockSpec((1, k.shape[1], BRANCH), lambda b, i: (b, 0, 0))
        in_specs += [spec, spec]
        args += [k, v]
    seg_rows = tuple(k.shape[1] for k, _ in segs)
    return pl.pallas_call(
        functools.partial(_diff_attn_kernel, seg_rows=seg_rows, lam_init=lam_init),
        grid=(bsz, n // tq),
        in_specs=in_specs,
        out_specs=qspec,
        out_shape=jax.ShapeDtypeStruct((bsz, n, BRANCH), BF16),
        scratch_shapes=[pltpu.VMEM((BRANCH // LANES, sum(seg_rows), 2 * LANES), BF16)],
        compiler_params=_params(2),
        name=f"diff_attn_{len(segs)}seg",
    )(*args)


def _out_proj_ln(left, right, w_ref, h, gate, lg, lb, alpha):
    y = jnp.dot(left, w_ref[:BRANCH, :], preferred_element_type=F32)
    y = y + jnp.dot(right, w_ref[BRANCH:, :], preferred_element_type=F32)
    return _layer_norm(alpha * h + gate * y, lg, lb)


def _ab_out_kernel(ta_ref, vn_ref, bm_ref, h_ref, gate_ref, ws_ref, bs_ref, w_ref, lg_ref, lb_ref, o_ref,
                   *, alpha):
    tm = ta_ref.shape[1]
    groups = BRANCH // LANES
    rows = []
    for c in range(tm // CHUNK):
        cols = []
        for g in range(groups):
            vc = vn_ref[0, c * CHUNK:(c + 1) * CHUNK, g * LANES:(g + 1) * LANES]
            cols.append(jnp.dot(ws_ref[g], vc, preferred_element_type=F32))
        rows.append(jnp.concatenate(cols, axis=-1) + bs_ref[...])
    mixed = rows[0] if len(rows) == 1 else jnp.concatenate(rows, axis=0)
    a_mix = (ta_ref[0].astype(F32) * mixed).astype(BF16)
    o_ref[0] = _out_proj_ln(a_mix, bm_ref[0], w_ref, h_ref[0], gate_ref[0], lg_ref[...], lb_ref[...], alpha)


def _ab_out(ta, vn, bm, h, mod, mod_row, ws, bs2d, w_out, lg, lb, alpha, tm):
    bsz, n, _ = h.shape
    bspec = pl.BlockSpec((1, tm, BRANCH), lambda b, i: (b, i, 0))
    hspec = pl.BlockSpec((1, tm, D_MODEL), lambda b, i: (b, i, 0))
    in_specs = [bspec, bspec, bspec, hspec] + _mod_specs(mod_row, (2,)) + [
        pl.BlockSpec(ws.shape, lambda b, i: (0, 0, 0)),
        pl.BlockSpec(bs2d.shape, lambda b, i: (0, 0)),
        pl.BlockSpec(w_out.shape, lambda b, i: (0, 0)),
        pl.BlockSpec((1, D_MODEL), lambda b, i: (0, 0)),
        pl.BlockSpec((1, D_MODEL), lambda b, i: (0, 0)),
    ]
    return pl.pallas_call(
        functools.partial(_ab_out_kernel, alpha=alpha),
        grid=(bsz, n // tm),
        in_specs=in_specs,
        out_specs=hspec,
        out_shape=jax.ShapeDtypeStruct((bsz, n, D_MODEL), F32),
        compiler_params=_params(2),
        name="ab_out",
    )(ta, vn, bm, h, mod, ws, bs2d, w_out, lg, lb)


def _cd_in_kernel(x_ref, shift_ref, scale_ref, w_ref, cos_ref, su_ref, sd_ref,
                  glu_ref, cg_ref, q_ref, k_ref, ks_ref, v_ref, vs_ref, dg_ref):
    u = (x_ref[0] * (1.0 + scale_ref[0]) + shift_ref[0]).astype(BF16)

    def cols(lo, width):
        return jnp.dot(u, w_ref[:, lo:lo + width], preferred_element_type=F32)

    glu_ref[0] = cols(0, BRANCH) * jax.nn.sigmoid(cols(BRANCH, BRANCH))
    cg_ref[0] = _silu(cols(2 * BRANCH, BRANCH)).astype(BF16)
    tabs = (cos_ref[...], su_ref[...], sd_ref[...])
    q_ref[0] = (_rope(cols(3 * BRANCH, BRANCH), *tabs) * Q_SCALE).astype(BF16)
    k = _rope(cols(4 * BRANCH, LANES), *tabs)
    v = cols(4 * BRANCH + LANES, LANES)
    k_ref[0] = k.astype(BF16)
    ks_ref[0] = pltpu.roll(k, HEAD_DIM, 1).astype(BF16)
    v_ref[0] = v.astype(BF16)
    vs_ref[0] = pltpu.roll(v, HEAD_DIM, 1).astype(BF16)
    dg_ref[0] = _silu(cols(4 * BRANCH + 2 * LANES, BRANCH)).astype(BF16)


def _cd_in(h, mod, mod_row, w_in, tables, tm):
    bsz, n, _ = h.shape
    in_specs = [pl.BlockSpec((1, tm, D_MODEL), lambda b, i: (b, i, 0))] + _mod_specs(mod_row, (0, 1))
    in_specs += [pl.BlockSpec(w_in.shape, lambda b, i: (0, 0))]
    in_specs += [pl.BlockSpec((tm, LANES), lambda b, i: (i, 0))] * 3
    wide = pl.BlockSpec((1, tm, BRANCH), lambda b, i: (b, i, 0))
    narrow = pl.BlockSpec((1, tm, LANES), lambda b, i: (b, i, 0))
    wide_bf = jax.ShapeDtypeStruct((bsz, n, BRANCH), BF16)
    narrow_bf = jax.ShapeDtypeStruct((bsz, n, LANES), BF16)
    return pl.pallas_call(
        _cd_in_kernel,
        grid=(bsz, n // tm),
        in_specs=in_specs,
        out_specs=[wide, wide, wide, narrow, narrow, narrow, narrow, wide],
        out_shape=[jax.ShapeDtypeStruct((bsz, n, BRANCH), F32), wide_bf, wide_bf,
                   narrow_bf, narrow_bf, narrow_bf, narrow_bf, wide_bf],
        compiler_params=_params(2),
        name="cd_in",
    )(h, mod, mod, w_in, *tables)


def _cd_ctx_kv_kernel(x_ref, shift_ref, scale_ref, w_ref, k_ref, ks_ref, v_ref, vs_ref):
    u = (x_ref[0] * (1.0 + scale_ref[0]) + shift_ref[0]).astype(BF16)
    kv = jnp.dot(u, w_ref[...], preferred_element_type=F32)
    k = kv[:, :LANES]
    v = kv[:, LANES:]
    k_ref[0] = k.astype(BF16)
    ks_ref[0] = pltpu.roll(k, HEAD_DIM, 1).astype(BF16)
    v_ref[0] = v.astype(BF16)
    vs_ref[0] = pltpu.roll(v, HEAD_DIM, 1).astype(BF16)


def _cd_ctx_kv(h, mod, mod_row, w_kv):
    bsz, n, _ = h.shape
    in_specs = [pl.BlockSpec((1, n, D_MODEL), lambda b: (b, 0, 0))]
    in_specs += [pl.BlockSpec((1, 1, D_MODEL), functools.partial(lambda b, p: (mod_row(b), 0, p), p=p))
                 for p in (0, 1)]
    in_specs += [pl.BlockSpec(w_kv.shape, lambda b: (0, 0))]
    spec = pl.BlockSpec((1, n, LANES), lambda b: (b, 0, 0))
    shape = jax.ShapeDtypeStruct((bsz, n, LANES), BF16)
    return pl.pallas_call(
        _cd_ctx_kv_kernel,
        grid=(bsz,),
        in_specs=in_specs,
        out_specs=[spec] * 4,
        out_shape=[shape] * 4,
        compiler_params=_params(1),
        name="cd_ctx_kv",
    )(h, mod, mod, w_kv)


def _swa_kernel(sink_ref, q_ref, dg_ref,
                kp_ref, kc_ref, kn_ref, ksp_ref, ksc_ref, ksn_ref,
                vp_ref, vc_ref, vn_ref, vsp_ref, vsc_ref, vsn_ref,
                ck_ref, cks_ref, cv_ref, cvs_ref, o_ref):
    i = pl.program_id(1)
    nsteps = pl.num_programs(1)
    n_slab = BRANCH // LANES
    stack = lambda a: jnp.concatenate([a] * n_slab, axis=0)
    r = lax.broadcasted_iota(jnp.int32, (CHUNK, CHUNK), 0)
    c = lax.broadcasted_iota(jnp.int32, (CHUNK, CHUNK), 1)
    tri_prev = stack(c >= r)
    tri_next = stack(c <= r)
    edge_prev = stack(c >= r + jnp.where(i > 0, 0, CHUNK))
    edge_next = stack(c <= r - jnp.where(i < nsteps - 1, 0, CHUNK))
    lo = _lane_mask((CHUNK, LANES), True)
    row = lax.broadcasted_iota(jnp.int32, (n_slab * CHUNK, 1), 0)
    blk = lambda t: slice(t * CHUNK, (t + 1) * CHUNK)

    layouts = (
        ((0, 2, 5, 7), (kp_ref, kc_ref, kn_ref, ck_ref), (vp_ref, vc_ref, vn_ref, cv_ref)),
        ((1, 3, 4, 6), (ksp_ref, ksc_ref, ksn_ref, cks_ref), (vsp_ref, vsc_ref, vsn_ref, cvs_ref)),
    )
    split = lambda a: [a[:, c0:c0 + LANES] for c0 in range(0, a.shape[1], LANES)]

    def band(t, refs):
        prev, cur, nxt, ctx = refs
        before = prev[0] if t == 0 else cur[0, blk(t - 1)]
        after = nxt[0] if t == SWA_BLOCKS - 1 else cur[0, blk(t + 1)]
        return before, cur[0, blk(t)], after, ctx[0]

    def scores(t, swapped):
        slabs = [q_ref[0, blk(t), j * LANES:(j + 1) * LANES] for j in range(n_slab)]
        zero = jnp.zeros_like(slabs[0])
        q_lo = [jnp.where(lo, q, zero) for q in slabs]
        q_hi = [jnp.where(lo, zero, q) for q in slabs]
        qs = jnp.concatenate((q_hi[:2] + q_lo[2:]) if swapped else (q_lo[:2] + q_hi[2:]), axis=0)
        masks = (edge_prev if t == 0 else tri_prev, None, edge_next if t == SWA_BLOCKS - 1 else tri_next, None)
        ss = [lax.dot_general(qs, kb, _NT, preferred_element_type=F32) for kb in band(t, layouts[swapped][1])]
        return [s if mk is None else jnp.where(mk, s, NEG_INF) for s, mk in zip(ss, masks)]

    def softmax(ss, swapped):
        sink = jnp.zeros((n_slab * CHUNK, 1), F32)
        for j, head in enumerate(layouts[swapped][0]):
            sink = jnp.where((row >= j * CHUNK) & (row < (j + 1) * CHUNK), sink_ref[head] * LOG2E, sink)
        m = jnp.max(functools.reduce(jnp.maximum, sum((split(s) for s in ss), [])), axis=-1, keepdims=True)
        m = jnp.maximum(m, sink)
        ps = [jnp.exp2(s - m) for s in ss]
        l = jnp.sum(functools.reduce(jnp.add, sum((split(p) for p in ps), [])), axis=-1, keepdims=True)
        l = l + jnp.exp2(sink - m)
        return [p.astype(BF16) for p in ps], 1.0 / l

    def weighted_values(ps, inv_l, t, swapped):
        acc = None
        for p, vb in zip(ps, band(t, layouts[swapped][2])):
            pv = jnp.dot(p, vb, preferred_element_type=F32)
            acc = pv if acc is None else acc + pv
        return acc * inv_l

    def emit(t, plain, swapped):
        out = []
        for j in range(n_slab):
            a = plain[j * CHUNK:(j + 1) * CHUNK]
            b = swapped[j * CHUNK:(j + 1) * CHUNK]
            out.append(jnp.where(lo, a, b) if j < 2 else jnp.where(lo, b, a))
        o = jnp.concatenate(out, axis=-1)
        o_ref[0, blk(t), :] = (o * dg_ref[0, blk(t), :].astype(F32)).astype(BF16)

    groups = [(t, swapped) for t in range(SWA_BLOCKS) for swapped in (0, 1)]
    ss_next = scores(*groups[0])
    normed = []
    for idx, (t, swapped) in enumerate(groups):
        ss = ss_next
        if idx + 1 < len(groups):
            ss_next = scores(*groups[idx + 1])
        ps, inv_l = softmax(ss, swapped)
        normed.append(weighted_values(ps, inv_l, t, swapped))
        if swapped:
            emit(t, normed[-2], normed[-1])


def _swa(sink, q, dg, k, ks, v, vs, ck, cks, cv, cvs):
    bsz, n, _ = q.shape
    nb = n // CHUNK
    tq = SWA_BLOCKS * CHUNK
    qspec = pl.BlockSpec((1, tq, BRANCH), lambda b, i: (b, i, 0))
    prev = pl.BlockSpec((1, CHUNK, LANES), lambda b, i: (b, jnp.maximum(SWA_BLOCKS * i - 1, 0), 0))
    cur = pl.BlockSpec((1, tq, LANES), lambda b, i: (b, i, 0))
    nxt = pl.BlockSpec((1, CHUNK, LANES), lambda b, i: (b, jnp.minimum(SWA_BLOCKS * (i + 1), nb - 1), 0))
    cspec = pl.BlockSpec((1, ck.shape[1], LANES), lambda b, i: (b, 0, 0))
    in_specs = [pl.BlockSpec(memory_space=pltpu.SMEM), qspec, qspec] + [prev, cur, nxt] * 4 + [cspec] * 4
    return pl.pallas_call(
        _swa_kernel,
        grid=(bsz, nb // SWA_BLOCKS),
        in_specs=in_specs,
        out_specs=qspec,
        out_shape=jax.ShapeDtypeStruct((bsz, n, BRANCH), BF16),
        compiler_params=_params(2),
        name="swa",
    )(sink, q, dg, k, k, k, ks, ks, ks, v, v, v, vs, vs, vs, ck, cks, cv, cvs)


def _cd_out_kernel(glu_ref, gp_ref, gn_ref, cg_ref, dm_ref, h_ref, gate_ref, dw_ref, db_ref, cng_ref, cnb_ref,
                   w_ref, lg_ref, lb_ref, o_ref, ext_ref, y_ref, *, alpha):
    i = pl.program_id(1)
    nb = pl.num_programs(1)
    tm = glu_ref.shape[1]
    ext_ref[0:CONV_HALO, :] = jnp.where(i > 0, gp_ref[0], 0.0)
    ext_ref[CONV_HALO:CONV_HALO + tm, :] = glu_ref[0]
    ext_ref[CONV_HALO + tm:, :] = jnp.where(i < nb - 1, gn_ref[0], 0.0)
    off = CONV_HALO - C_KERNEL // 2
    span = CONV_ROWS + 8

    def conv_rows(t, carry):
        r0 = pl.multiple_of(t * CONV_ROWS, CONV_ROWS)
        for c0 in range(0, BRANCH, LANES):
            win = ext_ref[pl.ds(r0, CONV_ROWS + 2 * CONV_HALO), c0:c0 + LANES]
            y = None
            for s in range(8):
                z = None
                for a in range((off + C_KERNEL + 7) // 8):
                    j = 8 * a + s - off
                    if 0 <= j < C_KERNEL:
                        term = win[8 * a:8 * a + span] * dw_ref[j:j + 1, c0:c0 + LANES]
                        z = term if z is None else z + term
                zs = z[s:s + CONV_ROWS]
                y = zs if y is None else y + zs
            y_ref[pl.ds(r0, CONV_ROWS), c0:c0 + LANES] = y
        return carry

    lax.fori_loop(0, tm // CONV_ROWS, conv_rows, 0)
    y = y_ref[...] + db_ref[...]
    c_lat = _silu(_layer_norm(y, cng_ref[...], cnb_ref[...]))
    c_mix = (c_lat * cg_ref[0].astype(F32)).astype(BF16)
    o_ref[0] = _out_proj_ln(c_mix, dm_ref[0], w_ref, h_ref[0], gate_ref[0], lg_ref[...], lb_ref[...], alpha)


def _cd_out(glu, cg, dm, h, mod, mod_row, dw_w, dw_b, cng, cnb, w_out, lg, lb, alpha, tm):
    bsz, n, _ = h.shape
    per = tm // CONV_HALO
    last = n // CONV_HALO - 1
    bspec = pl.BlockSpec((1, tm, BRANCH), lambda b, i: (b, i, 0))
    hspec = pl.BlockSpec((1, tm, D_MODEL), lambda b, i: (b, i, 0))
    vec = lambda width: pl.BlockSpec((1, width), lambda b, i: (0, 0))
    in_specs = [
        bspec,
        pl.BlockSpec((1, CONV_HALO, BRANCH), lambda b, i: (b, jnp.maximum(i * per - 1, 0), 0)),
        pl.BlockSpec((1, CONV_HALO, BRANCH), lambda b, i: (b, jnp.minimum((i + 1) * per, last), 0)),
        bspec, bspec, hspec,
    ] + _mod_specs(mod_row, (2,)) + [
        pl.BlockSpec(dw_w.shape, lambda b, i: (0, 0)),
        vec(BRANCH), vec(BRANCH), vec(BRANCH),
        pl.BlockSpec(w_out.shape, lambda b, i: (0, 0)),
        vec(D_MODEL), vec(D_MODEL),
    ]
    return pl.pallas_call(
        functools.partial(_cd_out_kernel, alpha=alpha),
        grid=(bsz, n // tm),
        in_specs=in_specs,
        out_specs=hspec,
        out_shape=jax.ShapeDtypeStruct((bsz, n, D_MODEL), F32),
        scratch_shapes=[pltpu.VMEM((tm + 2 * CONV_HALO, BRANCH), F32), pltpu.VMEM((tm, BRANCH), F32)],
        compiler_params=_params(2),
        name="cd_out",
    )(glu, glu, glu, cg, dm, h, mod, dw_w, dw_b, cng, cnb, w_out, lg, lb)


def kernel(x, c, ctx, c_ctx, mod_w, mod_b, ln_g, ln_b, ab_w_in, ab_w_out, a_w_s, a_b_s, a_norm_g, a_norm_b,
           b_lq1, b_lk1, b_lq2, b_lk2, b_subln_g, cd_w_in, cd_w_out, c_dw_w, c_dw_b, c_norm_g, c_norm_b, d_sink):
    bsz, n, _ = x.shape
    assert DEPTH == 2 and bsz + 1 <= MOD_ROWS
    alpha = (2.0 * DEPTH) ** 0.25
    tables = _rope_tables(n)

    cc = jnp.concatenate([c, c_ctx[None, :], jnp.zeros((MOD_ROWS - bsz - 1, D_MODEL), F32)], axis=0)
    mod = _modulation(cc, mod_w, mod_b).reshape(DEPTH * MOD_ROWS, 1, 3 * D_MODEL)
    lat_row = lambda layer: (lambda b: layer * MOD_ROWS + b)
    ctx_row = lambda layer: (lambda b: layer * MOD_ROWS + bsz)
    row2d = lambda a: a.reshape(1, -1)

    w_in0 = ab_w_in[0].astype(BF16)
    w_out0 = ab_w_out[0].astype(BF16)
    ws = a_w_s[0].astype(BF16)
    bs2d = jnp.repeat(a_b_s[0].T, LANES, axis=1)
    ng, nbias = row2d(a_norm_g[0]), row2d(a_norm_b[0])
    lvec = jnp.stack([b_lq1[0], b_lk1[0], b_lq2[0], b_lk2[0]])
    subln = row2d(b_subln_g[0])
    lam_init0 = 0.8 - 0.6 * math.exp(-0.3 * 0)
    lg0, lb0 = row2d(ln_g[0]), row2d(ln_b[0])

    ta, vn, q, k, v, sg = _ab_in(x, mod, lat_row(0), w_in0, ng, nbias, tables, tm=ROW_TILE)
    cta, cvn, cq, ck, cv, csg = _ab_in(ctx, mod, ctx_row(0), w_in0, ng, nbias, None, tm=CTX_LEN)
    bm = _diff_attn(lvec, subln, q, sg, [(ck, cv), (k, v)], lam_init0, tq=ATTN_TILE)
    cbm = _diff_attn(lvec, subln, cq, csg, [(ck, cv)], lam_init0, tq=CTX_LEN)
    h1 = _ab_out(ta, vn, bm, x, mod, lat_row(0), ws, bs2d, w_out0, lg0, lb0, alpha, tm=ROW_TILE)
    hc1 = _ab_out(cta, cvn, cbm, ctx, mod, ctx_row(0), ws, bs2d, w_out0, lg0, lb0, alpha, tm=CTX_LEN)

    w_in1 = cd_w_in[0].astype(BF16)
    w_out1 = cd_w_out[0].astype(BF16)
    kv_lo = 4 * BRANCH
    glu, cg, dq, dk, dks, dv, dvs, dg = _cd_in(h1, mod, lat_row(1), w_in1, tables, tm=ROW_TILE)
    ck1, cks1, cv1, cvs1 = _cd_ctx_kv(hc1, mod, ctx_row(1), w_in1[:, kv_lo:kv_lo + 2 * LANES])
    dm = _swa(d_sink[0], dq, dg, dk, dks, dv, dvs, ck1, cks1, cv1, cvs1)
    return _cd_out(glu, cg, dm, h1, mod, lat_row(1), c_dw_w[0], row2d(c_dw_b[0]), row2d(c_norm_g[0]),
                   row2d(c_norm_b[0]), w_out1, row2d(ln_g[1]), row2d(ln_b[1]), alpha, tm=ROW_TILE)
```

```python
import functools
import math

import jax
import jax.numpy as jnp
import numpy as np
from jax import lax
from jax.experimental import pallas as pl
from jax.experimental.pallas import tpu as pltpu

F32 = jnp.float32
BF16 = jnp.bfloat16

D_MODEL = 1024
DEPTH = 2
CTX_LEN = 256
GRID_W = 64
ROPE_THETA = 10000.0
LN_EPS = 1e-6
RMS_EPS = 1e-5
NEG_INF = -1e30
HEAD_DIM = 64
BRANCH = 512
CHUNK = 128
C_KERNEL = 31
CONV_HALO = 16
MOD_ROWS = 24
LANES = 128

VMEM_LIMIT = 56 * 1024 * 1024
LOG2E = math.log2(math.e)
Q_SCALE = HEAD_DIM ** -0.5 * LOG2E
KEY_CHUNK = 256
CONV_ROWS = 128
ROW_TILE = 1024
ATTN_TILE = 512
ATTN_SUB = 256
SWA_BLOCKS = 8

_NT = (((1,), (1,)), ((), ()))


def _params(n_axes, vmem=VMEM_LIMIT):
    return pltpu.CompilerParams(dimension_semantics=("arbitrary",) * n_axes, vmem_limit_bytes=vmem)


def _layer_norm(x, g, b):
    mu = jnp.mean(x, axis=-1, keepdims=True)
    xc = x - mu
    var = jnp.mean(xc * xc, axis=-1, keepdims=True)
    return xc * lax.rsqrt(var + LN_EPS) * g + b


def _lane_mask(shape, lo):
    lane = lax.broadcasted_iota(jnp.int32, shape, len(shape) - 1) % LANES
    return (lane < HEAD_DIM) if lo else (lane >= HEAD_DIM)


def _mod_kernel(c_ref, w_ref, b_ref, o_ref):
    c = c_ref[...]
    a = c * jax.nn.sigmoid(c)
    a_hi = a.astype(BF16)
    a_lo = (a - a_hi.astype(F32)).astype(BF16)
    w = w_ref[0]
    w_hi = w.astype(BF16)
    w_lo = (w - w_hi.astype(F32)).astype(BF16)
    acc = jnp.dot(a_hi, w_hi, preferred_element_type=F32)
    acc = acc + jnp.dot(a_hi, w_lo, preferred_element_type=F32)
    acc = acc + jnp.dot(a_lo, w_hi, preferred_element_type=F32)
    o_ref[0] = acc + b_ref[0]


def _modulation(cc, mod_w, mod_b):
    tn = 768
    n_out = 3 * D_MODEL
    return pl.pallas_call(
        _mod_kernel,
        grid=(DEPTH, n_out // tn),
        in_specs=[
            pl.BlockSpec((MOD_ROWS, D_MODEL), lambda l, j: (0, 0)),
            pl.BlockSpec((1, D_MODEL, tn), lambda l, j: (l, 0, j)),
            pl.BlockSpec((1, 1, tn), lambda l, j: (l, 0, j)),
        ],
        out_specs=pl.BlockSpec((1, MOD_ROWS, tn), lambda l, j: (l, 0, j)),
        out_shape=jax.ShapeDtypeStruct((DEPTH, MOD_ROWS, n_out), F32),
        compiler_params=_params(2),
        name="modulation",
    )(cc, mod_w, mod_b.reshape(DEPTH, 1, n_out))


def _mod_specs(row_fn, parts):
    return [pl.BlockSpec((1, 1, D_MODEL), functools.partial(lambda b, i, p: (row_fn(b), 0, p), p=p))
            for p in parts]


def _rope_tables(n):
    m = HEAD_DIM // 4
    inv = ROPE_THETA ** (-np.arange(m, dtype=np.float64) / m)
    t = np.arange(n)
    ang_r = (t // GRID_W)[:, None] * inv
    ang_c = (t % GRID_W)[:, None] * inv
    z = np.zeros_like(ang_r)
    cos = np.concatenate([np.cos(ang_r)] * 2 + [np.cos(ang_c)] * 2, axis=-1)
    sin_up = np.concatenate([-np.sin(ang_r), z, -np.sin(ang_c), z], axis=-1)
    sin_dn = np.concatenate([z, np.sin(ang_r), z, np.sin(ang_c)], axis=-1)
    tile = lambda a: jnp.asarray(np.concatenate([a, a], axis=-1), dtype=F32)
    return tile(cos), tile(sin_up), tile(sin_dn)


def _rope(x, cos, sin_up, sin_dn):
    outs = []
    for s in range(x.shape[-1] // LANES):
        xs = x[:, s * LANES:(s + 1) * LANES]
        up = pltpu.roll(xs, LANES - 16, 1)
        dn = pltpu.roll(xs, 16, 1)
        outs.append(xs * cos + up * sin_up + dn * sin_dn)
    return outs[0] if len(outs) == 1 else jnp.concatenate(outs, axis=-1)


def _silu(x):
    return x * jax.nn.sigmoid(x)


def _ab_in_kernel(*refs, rope):
    if rope:
        (x_ref, shift_ref, scale_ref, w_ref, ng_ref, nb_ref, cos_ref, su_ref, sd_ref,
         ta_ref, vn_ref, q_ref, k_ref, v_ref, sg_ref) = refs
    else:
        (x_ref, shift_ref, scale_ref, w_ref, ng_ref, nb_ref,
         ta_ref, vn_ref, q_ref, k_ref, v_ref, sg_ref) = refs
    u = (x_ref[0] * (1.0 + scale_ref[0]) + shift_ref[0]).astype(BF16)

    def seg(s):
        return jnp.dot(u, w_ref[:, s * BRANCH:(s + 1) * BRANCH], preferred_element_type=F32)

    ta_ref[0] = (jax.nn.gelu(seg(0)) * _silu(seg(2))).astype(BF16)
    vn_ref[0] = _layer_norm(jax.nn.gelu(seg(1)), ng_ref[...], nb_ref[...]).astype(BF16)
    q = seg(3)
    k = seg(4)
    if rope:
        tabs = (cos_ref[...], su_ref[...], sd_ref[...])
        q = _rope(q, *tabs)
        k = _rope(k, *tabs)
    q_ref[0] = (q * Q_SCALE).astype(BF16)
    k_ref[0] = k.astype(BF16)
    v_ref[0] = seg(5).astype(BF16)
    sg_ref[0] = _silu(seg(6)).astype(BF16)


def _ab_in(h, mod, mod_row, w_in, ng, nb, tables, tm):
    bsz, n, _ = h.shape
    rope = tables is not None
    in_specs = [pl.BlockSpec((1, tm, D_MODEL), lambda b, i: (b, i, 0))]
    in_specs += _mod_specs(mod_row, (0, 1))
    in_specs += [
        pl.BlockSpec(w_in.shape, lambda b, i: (0, 0)),
        pl.BlockSpec((1, BRANCH), lambda b, i: (0, 0)),
        pl.BlockSpec((1, BRANCH), lambda b, i: (0, 0)),
    ]
    args = [h, mod, mod, w_in, ng, nb]
    if rope:
        in_specs += [pl.BlockSpec((tm, LANES), lambda b, i: (i, 0))] * 3
        args += list(tables)
    out_spec = pl.BlockSpec((1, tm, BRANCH), lambda b, i: (b, i, 0))
    out_shape = jax.ShapeDtypeStruct((bsz, n, BRANCH), BF16)
    return pl.pallas_call(
        functools.partial(_ab_in_kernel, rope=rope),
        grid=(bsz, n // tm),
        in_specs=in_specs,
        out_specs=[out_spec] * 6,
        out_shape=[out_shape] * 6,
        compiler_params=_params(2),
        name="ab_in_rope" if rope else "ab_in_ctx",
    )(*args)


def _diff_attn_kernel(*refs, seg_rows, lam_init):
    n_seg = len(seg_rows)
    lv_ref, g_ref, q_ref, sg_ref = refs[:4]
    kv = refs[4:4 + 2 * n_seg]
    o_ref = refs[4 + 2 * n_seg]
    vext_ref = refs[5 + 2 * n_seg]
    heads = BRANCH // LANES

    @pl.when(pl.program_id(1) == 0)
    def _fill_values():
        for h in range(heads):
            base = 0
            for j, rows in enumerate(seg_rows):
                vext_ref[h, base:base + rows, :LANES] = kv[2 * j + 1][0, :, h * LANES:(h + 1) * LANES]
                base += rows
            vext_ref[h, :, LANES:] = jnp.ones((base, LANES), BF16)

    lv = lv_ref[...]
    lam = (jnp.exp(jnp.sum(lv[0:1] * lv[1:2], axis=-1, keepdims=True))
           - jnp.exp(jnp.sum(lv[2:3] * lv[3:4], axis=-1, keepdims=True)) + lam_init)
    tq = min(q_ref.shape[1], ATTN_SUB)
    lo = _lane_mask((tq, LANES), True)
    for r0, h in [(r0, h) for r0 in range(0, q_ref.shape[1], tq) for h in range(heads)]:
        cols = slice(h * LANES, (h + 1) * LANES)
        q = q_ref[0, r0:r0 + tq, cols]
        zero = jnp.zeros_like(q)
        qs = jnp.concatenate([jnp.where(lo, q, zero), jnp.where(lo, zero, q)], axis=0)
        m = acc = None
        base = 0
        for j, rows in enumerate(seg_rows):
            for c0 in range(0, rows, KEY_CHUNK):
                n = min(KEY_CHUNK, rows - c0)
                s = lax.dot_general(qs, kv[2 * j][0, c0:c0 + n, cols], _NT, preferred_element_type=F32)
                smax = functools.reduce(jnp.maximum, [s[:, l0:l0 + LANES] for l0 in range(0, n, LANES)])
                smax = jnp.max(smax, axis=-1, keepdims=True)
                m_new = smax if m is None else jnp.maximum(m, smax)
                p = jnp.exp2(s - m_new).astype(BF16)
                pv = jnp.dot(p, vext_ref[h, base + c0:base + c0 + n, :], preferred_element_type=F32)
                acc = pv if acc is None else jnp.exp2(m - m_new) * acc + pv
                m = m_new
            base += rows
        on = acc[:, :LANES] * (1.0 / acc[:, LANES:])
        o = on[:tq] - lam * on[tq:]
        o = o * lax.rsqrt(jnp.mean(o * o, axis=-1, keepdims=True) + RMS_EPS) * g_ref[...] * (1.0 - lam_init)
        o_ref[0, r0:r0 + tq, cols] = (o * sg_ref[0, r0:r0 + tq, cols].astype(F32)).astype(BF16)


def _diff_attn(lvec, subln_g, q, sg, segs, lam_init, tq):
    bsz, n, _ = q.shape
    qspec = pl.BlockSpec((1, tq, BRANCH), lambda b, i: (b, i, 0))
    in_specs = [
        pl.BlockSpec(lvec.shape, lambda b, i: (0, 0)),
        pl.BlockSpec((1, LANES), lambda b, i: (0, 0)),
        qspec, qspec,
    ]
    args = [lvec, subln_g, q, sg]
    for k, v in segs:
        spec = pl.BlockSpec((1, k.shape[1], BRANCH), lambda b, i: (b, 0, 0))
        in_specs += [spec, spec]
        args += [k, v]
    seg_rows = tuple(k.shape[1] for k, _ in segs)
    return pl.pallas_call(
        functools.partial(_diff_attn_kernel, seg_rows=seg_rows, lam_init=lam_init),
        grid=(bsz, n // tq),
        in_specs=in_specs,
        out_specs=qspec,
        out_shape=jax.ShapeDtypeStruct((bsz, n, BRANCH), BF16),
        scratch_shapes=[pltpu.VMEM((BRANCH // LANES, sum(seg_rows), 2 * LANES), BF16)],
        compiler_params=_params(2),
        name=f"diff_attn_{len(segs)}seg",
    )(*args)


def _out_proj_ln(left, right, w_ref, h, gate, lg, lb, alpha):
    y = jnp.dot(left, w_ref[:BRANCH, :], preferred_element_type=F32)
    y = y + jnp.dot(right, w_ref[BRANCH:, :], preferred_element_type=F32)
    return _layer_norm(alpha * h + gate * y, lg, lb)


def _ab_out_kernel(ta_ref, vn_ref, bm_ref, h_ref, gate_ref, ws_ref, bs_ref, w_ref, lg_ref, lb_ref, o_ref,
                   *, alpha):
    tm = ta_ref.shape[1]
    groups = BRANCH // LANES
    rows = []
    for c in range(tm // CHUNK):
        cols = []
        for g in range(groups):
            vc = vn_ref[0, c * CHUNK:(c + 1) * CHUNK, g * LANES:(g + 1) * LANES]
            cols.append(jnp.dot(ws_ref[g], vc, preferred_element_type=F32))
        rows.append(jnp.concatenate(cols, axis=-1) + bs_ref[...])
    mixed = rows[0] if len(rows) == 1 else jnp.concatenate(rows, axis=0)
    a_mix = (ta_ref[0].astype(F32) * mixed).astype(BF16)
    o_ref[0] = _out_proj_ln(a_mix, bm_ref[0], w_ref, h_ref[0], gate_ref[0], lg_ref[...], lb_ref[...], alpha)


def _ab_out(ta, vn, bm, h, mod, mod_row, ws, bs2d, w_out, lg, lb, alpha, tm):
    bsz, n, _ = h.shape
    bspec = pl.BlockSpec((1, tm, BRANCH), lambda b, i: (b, i, 0))
    hspec = pl.BlockSpec((1, tm, D_MODEL), lambda b, i: (b, i, 0))
    in_specs = [bspec, bspec, bspec, hspec] + _mod_specs(mod_row, (2,)) + [
        pl.BlockSpec(ws.shape, lambda b, i: (0, 0, 0)),
        pl.BlockSpec(bs2d.shape, lambda b, i: (0, 0)),
        pl.BlockSpec(w_out.shape, lambda b, i: (0, 0)),
        pl.BlockSpec((1, D_MODEL), lambda b, i: (0, 0)),
        pl.BlockSpec((1, D_MODEL), lambda b, i: (0, 0)),
    ]
    return pl.pallas_call(
        functools.partial(_ab_out_kernel, alpha=alpha),
        grid=(bsz, n // tm),
        in_specs=in_specs,
        out_specs=hspec,
        out_shape=jax.ShapeDtypeStruct((bsz, n, D_MODEL), F32),
        compiler_params=_params(2),
        name="ab_out",
    )(ta, vn, bm, h, mod, ws, bs2d, w_out, lg, lb)


def _cd_in_kernel(x_ref, shift_ref, scale_ref, w_ref, cos_ref, su_ref, sd_ref,
                  glu_ref, cg_ref, q_ref, k_ref, ks_ref, v_ref, vs_ref, dg_ref):
    u = (x_ref[0] * (1.0 + scale_ref[0]) + shift_ref[0]).astype(BF16)

    def cols(lo, width):
        return jnp.dot(u, w_ref[:, lo:lo + width], preferred_element_type=F32)

    glu_ref[0] = cols(0, BRANCH) * jax.nn.sigmoid(cols(BRANCH, BRANCH))
    cg_ref[0] = _silu(cols(2 * BRANCH, BRANCH)).astype(BF16)
    tabs = (cos_ref[...], su_ref[...], sd_ref[...])
    q_ref[0] = (_rope(cols(3 * BRANCH, BRANCH), *tabs) * Q_SCALE).astype(BF16)
    k = _rope(cols(4 * BRANCH, LANES), *tabs)
    v = cols(4 * BRANCH + LANES, LANES)
    k_ref[0] = k.astype(BF16)
    ks_ref[0] = pltpu.roll(k, HEAD_DIM, 1).astype(BF16)
    v_ref[0] = v.astype(BF16)
    vs_ref[0] = pltpu.roll(v, HEAD_DIM, 1).astype(BF16)
    dg_ref[0] = _silu(cols(4 * BRANCH + 2 * LANES, BRANCH)).astype(BF16)


def _cd_in(h, mod, mod_row, w_in, tables, tm):
    bsz, n, _ = h.shape
    in_specs = [pl.BlockSpec((1, tm, D_MODEL), lambda b, i: (b, i, 0))] + _mod_specs(mod_row, (0, 1))
    in_specs += [pl.BlockSpec(w_in.shape, lambda b, i: (0, 0))]
    in_specs += [pl.BlockSpec((tm, LANES), lambda b, i: (i, 0))] * 3
    wide = pl.BlockSpec((1, tm, BRANCH), lambda b, i: (b, i, 0))
    narrow = pl.BlockSpec((1, tm, LANES), lambda b, i: (b, i, 0))
    wide_bf = jax.ShapeDtypeStruct((bsz, n, BRANCH), BF16)
    narrow_bf = jax.ShapeDtypeStruct((bsz, n, LANES), BF16)
    return pl.pallas_call(
        _cd_in_kernel,
        grid=(bsz, n // tm),
        in_specs=in_specs,
        out_specs=[wide, wide, wide, narrow, narrow, narrow, narrow, wide],
        out_shape=[jax.ShapeDtypeStruct((bsz, n, BRANCH), F32), wide_bf, wide_bf,
                   narrow_bf, narrow_bf, narrow_bf, narrow_bf, wide_bf],
        compiler_params=_params(2),
        name="cd_in",
    )(h, mod, mod, w_in, *tables)


def _cd_ctx_kv_kernel(x_ref, shift_ref, scale_ref, w_ref, k_ref, ks_ref, v_ref, vs_ref):
    u = (x_ref[0] * (1.0 + scale_ref[0]) + shift_ref[0]).astype(BF16)
    kv = jnp.dot(u, w_ref[...], preferred_element_type=F32)
    k = kv[:, :LANES]
    v = kv[:, LANES:]
    k_ref[0] = k.astype(BF16)
    ks_ref[0] = pltpu.roll(k, HEAD_DIM, 1).astype(BF16)
    v_ref[0] = v.astype(BF16)
    vs_ref[0] = pltpu.roll(v, HEAD_DIM, 1).astype(BF16)


def _cd_ctx_kv(h, mod, mod_row, w_kv):
    bsz, n, _ = h.shape
    in_specs = [pl.BlockSpec((1, n, D_MODEL), lambda b: (b, 0, 0))]
    in_specs += [pl.BlockSpec((1, 1, D_MODEL), functools.partial(lambda b, p: (mod_row(b), 0, p), p=p))
                 for p in (0, 1)]
    in_specs += [pl.BlockSpec(w_kv.shape, lambda b: (0, 0))]
    spec = pl.BlockSpec((1, n, LANES), lambda b: (b, 0, 0))
    shape = jax.ShapeDtypeStruct((bsz, n, LANES), BF16)
    return pl.pallas_call(
        _cd_ctx_kv_kernel,
        grid=(bsz,),
        in_specs=in_specs,
        out_specs=[spec] * 4,
        out_shape=[shape] * 4,
        compiler_params=_params(1),
        name="cd_ctx_kv",
    )(h, mod, mod, w_kv)


def _swa_kernel(sink_ref, q_ref, dg_ref,
                kp_ref, kc_ref, kn_ref, ksp_ref, ksc_ref, ksn_ref,
                vp_ref, vc_ref, vn_ref, vsp_ref, vsc_ref, vsn_ref,
                ck_ref, cks_ref, cv_ref, cvs_ref, o_ref):
    i = pl.program_id(1)
    nsteps = pl.num_programs(1)
    n_slab = BRANCH // LANES
    stack = lambda a: jnp.concatenate([a] * n_slab, axis=0)
    r = lax.broadcasted_iota(jnp.int32, (CHUNK, CHUNK), 0)
    c = lax.broadcasted_iota(jnp.int32, (CHUNK, CHUNK), 1)
    tri_prev = stack(c >= r)
    tri_next = stack(c <= r)
    edge_prev = stack(c >= r + jnp.where(i > 0, 0, CHUNK))
    edge_next = stack(c <= r - jnp.where(i < nsteps - 1, 0, CHUNK))
    lo = _lane_mask((CHUNK, LANES), True)
    row = lax.broadcasted_iota(jnp.int32, (n_slab * CHUNK, 1), 0)
    blk = lambda t: slice(t * CHUNK, (t + 1) * CHUNK)

    layouts = (
        ((0, 2, 5, 7), (kp_ref, kc_ref, kn_ref, ck_ref), (vp_ref, vc_ref, vn_ref, cv_ref)),
        ((1, 3, 4, 6), (ksp_ref, ksc_ref, ksn_ref, cks_ref), (vsp_ref, vsc_ref, vsn_ref, cvs_ref)),
    )
    split = lambda a: [a[:, c0:c0 + LANES] for c0 in range(0, a.shape[1], LANES)]

    def band(t, refs):
        prev, cur, nxt, ctx = refs
        before = prev[0] if t == 0 else cur[0, blk(t - 1)]
        after = nxt[0] if t == SWA_BLOCKS - 1 else cur[0, blk(t + 1)]
        return before, cur[0, blk(t)], after, ctx[0]

    def scores(t, swapped):
        slabs = [q_ref[0, blk(t), j * LANES:(j + 1) * LANES] for j in range(n_slab)]
        zero = jnp.zeros_like(slabs[0])
        q_lo = [jnp.where(lo, q, zero) for q in slabs]
        q_hi = [jnp.where(lo, zero, q) for q in slabs]
        qs = jnp.concatenate((q_hi[:2] + q_lo[2:]) if swapped else (q_lo[:2] + q_hi[2:]), axis=0)
        masks = (edge_prev if t == 0 else tri_prev, None, edge_next if t == SWA_BLOCKS - 1 else tri_next, None)
        ss = [lax.dot_general(qs, kb, _NT, preferred_element_type=F32) for kb in band(t, layouts[swapped][1])]
        return [s if mk is None else jnp.where(mk, s, NEG_INF) for s, mk in zip(ss, masks)]

    def softmax(ss, swapped):
        sink = jnp.zeros((n_slab * CHUNK, 1), F32)
        for j, head in enumerate(layouts[swapped][0]):
            sink = jnp.where((row >= j * CHUNK) & (row < (j + 1) * CHUNK), sink_ref[head] * LOG2E, sink)
        m = jnp.max(functools.reduce(jnp.maximum, sum((split(s) for s in ss), [])), axis=-1, keepdims=True)
        m = jnp.maximum(m, sink)
        ps = [jnp.exp2(s - m) for s in ss]
        l = jnp.sum(functools.reduce(jnp.add, sum((split(p) for p in ps), [])), axis=-1, keepdims=True)
        l = l + jnp.exp2(sink - m)
        return [p.astype(BF16) for p in ps], 1.0 / l

    def weighted_values(ps, inv_l, t, swapped):
        acc = None
        for p, vb in zip(ps, band(t, layouts[swapped][2])):
            pv = jnp.dot(p, vb, preferred_element_type=F32)
            acc = pv if acc is None else acc + pv
        return acc * inv_l

    def emit(t, plain, swapped):
        out = []
        for j in range(n_slab):
            a = plain[j * CHUNK:(j + 1) * CHUNK]
            b = swapped[j * CHUNK:(j + 1) * CHUNK]
            out.append(jnp.where(lo, a, b) if j < 2 else jnp.where(lo, b, a))
        o = jnp.concatenate(out, axis=-1)
        o_ref[0, blk(t), :] = (o * dg_ref[0, blk(t), :].astype(F32)).astype(BF16)

    groups = [(t, swapped) for t in range(SWA_BLOCKS) for swapped in (0, 1)]
    ss_next = scores(*groups[0])
    normed = []
    for idx, (t, swapped) in enumerate(groups):
        ss = ss_next
        if idx + 1 < len(groups):
            ss_next = scores(*groups[idx + 1])
        ps, inv_l = softmax(ss, swapped)
        normed.append(weighted_values(ps, inv_l, t, swapped))
        if swapped:
            emit(t, normed[-2], normed[-1])


def _swa(sink, q, dg, k, ks, v, vs, ck, cks, cv, cvs):
    bsz, n, _ = q.shape
    nb = n // CHUNK
    tq = SWA_BLOCKS * CHUNK
    qspec = pl.BlockSpec((1, tq, BRANCH), lambda b, i: (b, i, 0))
    prev = pl.BlockSpec((1, CHUNK, LANES), lambda b, i: (b, jnp.maximum(SWA_BLOCKS * i - 1, 0), 0))
    cur = pl.BlockSpec((1, tq, LANES), lambda b, i: (b, i, 0))
    nxt = pl.BlockSpec((1, CHUNK, LANES), lambda b, i: (b, jnp.minimum(SWA_BLOCKS * (i + 1), nb - 1), 0))
    cspec = pl.BlockSpec((1, ck.shape[1], LANES), lambda b, i: (b, 0, 0))
    in_specs = [pl.BlockSpec(memory_space=pltpu.SMEM), qspec, qspec] + [prev, cur, nxt] * 4 + [cspec] * 4
    return pl.pallas_call(
        _swa_kernel,
        grid=(bsz, nb // SWA_BLOCKS),
        in_specs=in_specs,
        out_specs=qspec,
        out_shape=jax.ShapeDtypeStruct((bsz, n, BRANCH), BF16),
        compiler_params=_params(2),
        name="swa",
    )(sink, q, dg, k, k, k, ks, ks, ks, v, v, v, vs, vs, vs, ck, cks, cv, cvs)


def _cd_out_kernel(glu_ref, gp_ref, gn_ref, cg_ref, dm_ref, h_ref, gate_ref, dw_ref, db_ref, cng_ref, cnb_ref,
                   w_ref, lg_ref, lb_ref, o_ref, ext_ref, y_ref, *, alpha):
    i = pl.program_id(1)
    nb = pl.num_programs(1)
    tm = glu_ref.shape[1]
    ext_ref[0:CONV_HALO, :] = jnp.where(i > 0, gp_ref[0], 0.0)
    ext_ref[CONV_HALO:CONV_HALO + tm, :] = glu_ref[0]
    ext_ref[CONV_HALO + tm:, :] = jnp.where(i < nb - 1, gn_ref[0], 0.0)
    off = CONV_HALO - C_KERNEL // 2
    span = CONV_ROWS + 8

    def conv_rows(t, carry):
        r0 = pl.multiple_of(t * CONV_ROWS, CONV_ROWS)
        for c0 in range(0, BRANCH, LANES):
            win = ext_ref[pl.ds(r0, CONV_ROWS + 2 * CONV_HALO), c0:c0 + LANES]
            y = None
            for s in range(8):
                z = None
                for a in range((off + C_KERNEL + 7) // 8):
                    j = 8 * a + s - off
                    if 0 <= j < C_KERNEL:
                        term = win[8 * a:8 * a + span] * dw_ref[j:j + 1, c0:c0 + LANES]
                        z = term if z is None else z + term
                zs = z[s:s + CONV_ROWS]
                y = zs if y is None else y + zs
            y_ref[pl.ds(r0, CONV_ROWS), c0:c0 + LANES] = y
        return carry

    lax.fori_loop(0, tm // CONV_ROWS, conv_rows, 0)
    y = y_ref[...] + db_ref[...]
    c_lat = _silu(_layer_norm(y, cng_ref[...], cnb_ref[...]))
    c_mix = (c_lat * cg_ref[0].astype(F32)).astype(BF16)
    o_ref[0] = _out_proj_ln(c_mix, dm_ref[0], w_ref, h_ref[0], gate_ref[0], lg_ref[...], lb_ref[...], alpha)


def _cd_out(glu, cg, dm, h, mod, mod_row, dw_w, dw_b, cng, cnb, w_out, lg, lb, alpha, tm):
    bsz, n, _ = h.shape
    per = tm // CONV_HALO
    last = n // CONV_HALO - 1
    bspec = pl.BlockSpec((1, tm, BRANCH), lambda b, i: (b, i, 0))
    hspec = pl.BlockSpec((1, tm, D_MODEL), lambda b, i: (b, i, 0))
    vec = lambda width: pl.BlockSpec((1, width), lambda b, i: (0, 0))
    in_specs = [
        bspec,
        pl.BlockSpec((1, CONV_HALO, BRANCH), lambda b, i: (b, jnp.maximum(i * per - 1, 0), 0)),
        pl.BlockSpec((1, CONV_HALO, BRANCH), lambda b, i: (b, jnp.minimum((i + 1) * per, last), 0)),
        bspec, bspec, hspec,
    ] + _mod_specs(mod_row, (2,)) + [
        pl.BlockSpec(dw_w.shape, lambda b, i: (0, 0)),
        vec(BRANCH), vec(BRANCH), vec(BRANCH),
        pl.BlockSpec(w_out.shape, lambda b, i: (0, 0)),
        vec(D_MODEL), vec(D_MODEL),
    ]
    return pl.pallas_call(
        functools.partial(_cd_out_kernel, alpha=alpha),
        grid=(bsz, n // tm),
        in_specs=in_specs,
        out_specs=hspec,
        out_shape=jax.ShapeDtypeStruct((bsz, n, D_MODEL), F32),
        scratch_shapes=[pltpu.VMEM((tm + 2 * CONV_HALO, BRANCH), F32), pltpu.VMEM((tm, BRANCH), F32)],
        compiler_params=_params(2),
        name="cd_out",
    )(glu, glu, glu, cg, dm, h, mod, dw_w, dw_b, cng, cnb, w_out, lg, lb)


def kernel(x, c, ctx, c_ctx, mod_w, mod_b, ln_g, ln_b, ab_w_in, ab_w_out, a_w_s, a_b_s, a_norm_g, a_norm_b,
           b_lq1, b_lk1, b_lq2, b_lk2, b_subln_g, cd_w_in, cd_w_out, c_dw_w, c_dw_b, c_norm_g, c_norm_b, d_sink):
    bsz, n, _ = x.shape
    assert DEPTH == 2 and bsz + 1 <= MOD_ROWS
    alpha = (2.0 * DEPTH) ** 0.25
    tables = _rope_tables(n)

    cc = jnp.concatenate([c, c_ctx[None, :], jnp.zeros((MOD_ROWS - bsz - 1, D_MODEL), F32)], axis=0)
    mod = _modulation(cc, mod_w, mod_b).reshape(DEPTH * MOD_ROWS, 1, 3 * D_MODEL)
    lat_row = lambda layer: (lambda b: layer * MOD_ROWS + b)
    ctx_row = lambda layer: (lambda b: layer * MOD_ROWS + bsz)
    row2d = lambda a: a.reshape(1, -1)

    w_in0 = ab_w_in[0].astype(BF16)
    w_out0 = ab_w_out[0].astype(BF16)
    ws = a_w_s[0].astype(BF16)
    bs2d = jnp.repeat(a_b_s[0].T, LANES, axis=1)
    ng, nbias = row2d(a_norm_g[0]), row2d(a_norm_b[0])
    lvec = jnp.stack([b_lq1[0], b_lk1[0], b_lq2[0], b_lk2[0]])
    subln = row2d(b_subln_g[0])
    lam_init0 = 0.8 - 0.6 * math.exp(-0.3 * 0)
    lg0, lb0 = row2d(ln_g[0]), row2d(ln_b[0])

    ctx_len = ctx.shape[1]
    assert (bsz * ctx_len) % ROW_TILE == 0 and ROW_TILE % ctx_len == 0
    tiled = lambda a: a.reshape(-1, ROW_TILE, a.shape[-1])
    per_batch = lambda a: a.reshape(bsz, ctx_len, a.shape[-1])

    ta, vn, q, k, v, sg = _ab_in(x, mod, lat_row(0), w_in0, ng, nbias, tables, tm=ROW_TILE)
    cta, cvn, cq, ck, cv, csg = _ab_in(tiled(ctx), mod, ctx_row(0), w_in0, ng, nbias, None, tm=ROW_TILE)
    cq, ck, cv, csg = map(per_batch, (cq, ck, cv, csg))
    bm = _diff_attn(lvec, subln, q, sg, [(ck, cv), (k, v)], lam_init0, tq=ATTN_TILE)
    cbm = _diff_attn(lvec, subln, cq, csg, [(ck, cv)], lam_init0, tq=ctx_len)
    h1 = _ab_out(ta, vn, bm, x, mod, lat_row(0), ws, bs2d, w_out0, lg0, lb0, alpha, tm=ROW_TILE)
    hc1 = _ab_out(cta, cvn, tiled(cbm), tiled(ctx), mod, ctx_row(0), ws, bs2d, w_out0, lg0, lb0, alpha,
                  tm=ROW_TILE)

    w_in1 = cd_w_in[0].astype(BF16)
    w_out1 = cd_w_out[0].astype(BF16)
    kv_lo = 4 * BRANCH
    glu, cg, dq, dk, dks, dv, dvs, dg = _cd_in(h1, mod, lat_row(1), w_in1, tables, tm=ROW_TILE)
    ck1, cks1, cv1, cvs1 = map(per_batch, _cd_ctx_kv(hc1, mod, ctx_row(1), w_in1[:, kv_lo:kv_lo + 2 * LANES]))
    dm = _swa(d_sink[0], dq, dg, dk, dks, dv, dvs, ck1, cks1, cv1, cvs1)
    return _cd_out(glu, cg, dm, h1, mod, lat_row(1), c_dw_w[0], row2d(c_dw_b[0]), row2d(c_norm_g[0]),
                   row2d(c_norm_b[0]), w_out1, row2d(ln_g[1]), row2d(ln_b[1]), alpha, tm=ROW_TILE)
```

```python
import functools
import math

import jax
import jax.numpy as jnp
import numpy as np
from jax import lax
from jax.experimental import pallas as pl
from jax.experimental.pallas import tpu as pltpu

F32 = jnp.float32
BF16 = jnp.bfloat16

D_MODEL = 1024
DEPTH = 2
CTX_LEN = 256
GRID_W = 64
ROPE_THETA = 10000.0
LN_EPS = 1e-6
RMS_EPS = 1e-5
NEG_INF = -1e30
HEAD_DIM = 64
BRANCH = 512
CHUNK = 128
C_KERNEL = 31
CONV_HALO = 16
MOD_ROWS = 24
LANES = 128

VMEM_LIMIT = 56 * 1024 * 1024
LOG2E = math.log2(math.e)
Q_SCALE = HEAD_DIM ** -0.5 * LOG2E
KEY_CHUNK = 256
CONV_ROWS = 128
ROW_TILE = 1024
ATTN_TILE = 512
ATTN_SUB = 256
SWA_BLOCKS = 8

_NT = (((1,), (1,)), ((), ()))


def _params(n_axes, vmem=VMEM_LIMIT):
    return pltpu.CompilerParams(dimension_semantics=("arbitrary",) * n_axes, vmem_limit_bytes=vmem)


def _layer_norm(x, g, b):
    mu = jnp.mean(x, axis=-1, keepdims=True)
    xc = x - mu
    var = jnp.mean(xc * xc, axis=-1, keepdims=True)
    return xc * lax.rsqrt(var + LN_EPS) * g + b


def _lane_mask(shape, lo):
    lane = lax.broadcasted_iota(jnp.int32, shape, len(shape) - 1) % LANES
    return (lane < HEAD_DIM) if lo else (lane >= HEAD_DIM)


def _mod_kernel(c_ref, w_ref, b_ref, o_ref):
    c = c_ref[...]
    a = c * jax.nn.sigmoid(c)
    a_hi = a.astype(BF16)
    a_lo = (a - a_hi.astype(F32)).astype(BF16)
    w = w_ref[0]
    w_hi = w.astype(BF16)
    w_lo = (w - w_hi.astype(F32)).astype(BF16)
    acc = jnp.dot(a_hi, w_hi, preferred_element_type=F32)
    acc = acc + jnp.dot(a_hi, w_lo, preferred_element_type=F32)
    acc = acc + jnp.dot(a_lo, w_hi, preferred_element_type=F32)
    o_ref[0] = acc + b_ref[0]


def _modulation(cc, mod_w, mod_b):
    tn = 768
    n_out = 3 * D_MODEL
    return pl.pallas_call(
        _mod_kernel,
        grid=(DEPTH, n_out // tn),
        in_specs=[
            pl.BlockSpec((MOD_ROWS, D_MODEL), lambda l, j: (0, 0)),
            pl.BlockSpec((1, D_MODEL, tn), lambda l, j: (l, 0, j)),
            pl.BlockSpec((1, 1, tn), lambda l, j: (l, 0, j)),
        ],
        out_specs=pl.BlockSpec((1, MOD_ROWS, tn), lambda l, j: (l, 0, j)),
        out_shape=jax.ShapeDtypeStruct((DEPTH, MOD_ROWS, n_out), F32),
        compiler_params=_params(2),
        name="modulation",
    )(cc, mod_w, mod_b.reshape(DEPTH, 1, n_out))


def _mod_specs(row_fn, parts):
    return [pl.BlockSpec((1, 1, D_MODEL), functools.partial(lambda b, i, p: (row_fn(b), 0, p), p=p))
            for p in parts]


def _rope_tables(n):
    m = HEAD_DIM // 4
    inv = ROPE_THETA ** (-np.arange(m, dtype=np.float64) / m)
    t = np.arange(n)
    ang_r = (t // GRID_W)[:, None] * inv
    ang_c = (t % GRID_W)[:, None] * inv
    z = np.zeros_like(ang_r)
    cos = np.concatenate([np.cos(ang_r)] * 2 + [np.cos(ang_c)] * 2, axis=-1)
    sin_up = np.concatenate([-np.sin(ang_r), z, -np.sin(ang_c), z], axis=-1)
    sin_dn = np.concatenate([z, np.sin(ang_r), z, np.sin(ang_c)], axis=-1)
    tile = lambda a: jnp.asarray(np.concatenate([a, a], axis=-1), dtype=F32)
    return tile(cos), tile(sin_up), tile(sin_dn)


def _rope(x, cos, sin_up, sin_dn):
    outs = []
    for s in range(x.shape[-1] // LANES):
        xs = x[:, s * LANES:(s + 1) * LANES]
        up = pltpu.roll(xs, LANES - 16, 1)
        dn = pltpu.roll(xs, 16, 1)
        outs.append(xs * cos + up * sin_up + dn * sin_dn)
    return outs[0] if len(outs) == 1 else jnp.concatenate(outs, axis=-1)


def _silu(x):
    return x * jax.nn.sigmoid(x)


def _ab_in_kernel(*refs, rope):
    if rope:
        (x_ref, shift_ref, scale_ref, w_ref, ng_ref, nb_ref, cos_ref, su_ref, sd_ref,
         ta_ref, vn_ref, q_ref, k_ref, v_ref, sg_ref) = refs
    else:
        (x_ref, shift_ref, scale_ref, w_ref, ng_ref, nb_ref,
         ta_ref, vn_ref, q_ref, k_ref, v_ref, sg_ref) = refs
    u = (x_ref[0] * (1.0 + scale_ref[0]) + shift_ref[0]).astype(BF16)

    def seg(s):
        return jnp.dot(u, w_ref[:, s * BRANCH:(s + 1) * BRANCH], preferred_element_type=F32)

    ta_ref[0] = (jax.nn.gelu(seg(0)) * _silu(seg(2))).astype(BF16)
    vn_ref[0] = _layer_norm(jax.nn.gelu(seg(1)), ng_ref[...], nb_ref[...]).astype(BF16)
    q = seg(3)
    k = seg(4)
    if rope:
        tabs = (cos_ref[...], su_ref[...], sd_ref[...])
        q = _rope(q, *tabs)
        k = _rope(k, *tabs)
    q_ref[0] = (q * Q_SCALE).astype(BF16)
    k_ref[0] = k.astype(BF16)
    v_ref[0] = seg(5).astype(BF16)
    sg_ref[0] = _silu(seg(6)).astype(BF16)


def _ab_in(h, mod, mod_row, w_in, ng, nb, tables, tm):
    bsz, n, _ = h.shape
    rope = tables is not None
    in_specs = [pl.BlockSpec((1, tm, D_MODEL), lambda b, i: (b, i, 0))]
    in_specs += _mod_specs(mod_row, (0, 1))
    in_specs += [
        pl.BlockSpec(w_in.shape, lambda b, i: (0, 0)),
        pl.BlockSpec((1, BRANCH), lambda b, i: (0, 0)),
        pl.BlockSpec((1, BRANCH), lambda b, i: (0, 0)),
    ]
    args = [h, mod, mod, w_in, ng, nb]
    if rope:
        in_specs += [pl.BlockSpec((tm, LANES), lambda b, i: (i, 0))] * 3
        args += list(tables)
    out_spec = pl.BlockSpec((1, tm, BRANCH), lambda b, i: (b, i, 0))
    out_shape = jax.ShapeDtypeStruct((bsz, n, BRANCH), BF16)
    return pl.pallas_call(
        functools.partial(_ab_in_kernel, rope=rope),
        grid=(bsz, n // tm),
        in_specs=in_specs,
        out_specs=[out_spec] * 6,
        out_shape=[out_shape] * 6,
        compiler_params=_params(2),
        name="ab_in_rope" if rope else "ab_in_ctx",
    )(*args)


def _out_proj_ln(left, right, w_ref, h, gate, lg, lb, alpha):
    y = jnp.dot(left, w_ref[:BRANCH, :], preferred_element_type=F32)
    y = y + jnp.dot(right, w_ref[BRANCH:, :], preferred_element_type=F32)
    return _layer_norm(alpha * h + gate * y, lg, lb)


def _ab_attn_out_kernel(*refs, seg_rows, lam_init, alpha):
    n_seg = len(seg_rows)
    (lv_ref, g_ref, q_ref, sg_ref, ta_ref, vn_ref, h_ref, gate_ref, ws_ref, bs_ref, w_ref, lg_ref,
     lb_ref) = refs[:13]
    kv = refs[13:13 + 2 * n_seg]
    o_ref = refs[13 + 2 * n_seg]
    vext_ref = refs[14 + 2 * n_seg]
    bm_ref = refs[15 + 2 * n_seg]
    heads = BRANCH // LANES

    @pl.when(pl.program_id(1) == 0)
    def _fill_values():
        for h in range(heads):
            base = 0
            for j, rows in enumerate(seg_rows):
                vext_ref[h, base:base + rows, :LANES] = kv[2 * j + 1][0, :, h * LANES:(h + 1) * LANES]
                base += rows
            vext_ref[h, :, LANES:] = jnp.ones((base, LANES), BF16)

    lv = lv_ref[...]
    lam = (jnp.exp(jnp.sum(lv[0:1] * lv[1:2], axis=-1, keepdims=True))
           - jnp.exp(jnp.sum(lv[2:3] * lv[3:4], axis=-1, keepdims=True)) + lam_init)
    tq = min(q_ref.shape[1], ATTN_SUB)
    lo = _lane_mask((tq, LANES), True)

    def attend(r0, h):
        cols = slice(h * LANES, (h + 1) * LANES)
        q = q_ref[0, r0:r0 + tq, cols]
        zero = jnp.zeros_like(q)
        qs = jnp.concatenate([jnp.where(lo, q, zero), jnp.where(lo, zero, q)], axis=0)
        m = acc = None
        base = 0
        for j, rows in enumerate(seg_rows):
            for c0 in range(0, rows, KEY_CHUNK):
                n = min(KEY_CHUNK, rows - c0)
                s = lax.dot_general(qs, kv[2 * j][0, c0:c0 + n, cols], _NT, preferred_element_type=F32)
                smax = functools.reduce(jnp.maximum, [s[:, l0:l0 + LANES] for l0 in range(0, n, LANES)])
                smax = jnp.max(smax, axis=-1, keepdims=True)
                m_new = smax if m is None else jnp.maximum(m, smax)
                p = jnp.exp2(s - m_new).astype(BF16)
                pv = jnp.dot(p, vext_ref[h, base + c0:base + c0 + n, :], preferred_element_type=F32)
                acc = pv if acc is None else jnp.exp2(m - m_new) * acc + pv
                m = m_new
            base += rows
        on = acc[:, :LANES] * (1.0 / acc[:, LANES:])
        o = on[:tq] - lam * on[tq:]
        o = o * lax.rsqrt(jnp.mean(o * o, axis=-1, keepdims=True) + RMS_EPS) * g_ref[...] * (1.0 - lam_init)
        bm_ref[r0:r0 + tq, cols] = (o * sg_ref[0, r0:r0 + tq, cols].astype(F32)).astype(BF16)

    def mix_and_project(r0):
        rows = []
        for c0 in range(r0, r0 + tq, CHUNK):
            cols = [jnp.dot(ws_ref[g], vn_ref[0, c0:c0 + CHUNK, g * LANES:(g + 1) * LANES],
                            preferred_element_type=F32) for g in range(heads)]
            rows.append(jnp.concatenate(cols, axis=-1) + bs_ref[...])
        mixed = rows[0] if len(rows) == 1 else jnp.concatenate(rows, axis=0)
        a_mix = (ta_ref[0, r0:r0 + tq, :].astype(F32) * mixed).astype(BF16)
        o_ref[0, r0:r0 + tq, :] = _out_proj_ln(a_mix, bm_ref[r0:r0 + tq, :], w_ref, h_ref[0, r0:r0 + tq, :],
                                               gate_ref[0], lg_ref[...], lb_ref[...], alpha)

    starts = list(range(0, q_ref.shape[1], tq))
    for idx, r0 in enumerate(starts):
        for h in range(heads):
            attend(r0, h)
            if h == 0 and idx > 0:
                mix_and_project(starts[idx - 1])
    mix_and_project(starts[-1])


def _ab_attn_out(lvec, subln_g, q, sg, ta, vn, h, mod, mod_row, ws, bs2d, w_out, lg, lb, segs, lam_init, alpha, tq):
    bsz, n, _ = q.shape
    qspec = pl.BlockSpec((1, tq, BRANCH), lambda b, i: (b, i, 0))
    hspec = pl.BlockSpec((1, tq, D_MODEL), lambda b, i: (b, i, 0))
    whole = lambda a: pl.BlockSpec(a.shape, lambda b, i: (0,) * a.ndim)
    in_specs = [whole(lvec), whole(subln_g), qspec, qspec, qspec, qspec, hspec] + _mod_specs(mod_row, (2,)) + [
        whole(ws), whole(bs2d), whole(w_out), whole(lg), whole(lb)]
    args = [lvec, subln_g, q, sg, ta, vn, h, mod, ws, bs2d, w_out, lg, lb]
    for k, v in segs:
        spec = pl.BlockSpec((1, k.shape[1], BRANCH), lambda b, i: (b, 0, 0))
        in_specs += [spec, spec]
        args += [k, v]
    seg_rows = tuple(k.shape[1] for k, _ in segs)
    return pl.pallas_call(
        functools.partial(_ab_attn_out_kernel, seg_rows=seg_rows, lam_init=lam_init, alpha=alpha),
        grid=(bsz, n // tq),
        in_specs=in_specs,
        out_specs=hspec,
        out_shape=jax.ShapeDtypeStruct((bsz, n, D_MODEL), F32),
        scratch_shapes=[pltpu.VMEM((BRANCH // LANES, sum(seg_rows), 2 * LANES), BF16),
                        pltpu.VMEM((tq, BRANCH), BF16)],
        compiler_params=_params(2),
        name=f"ab_attn_out_{len(segs)}seg",
    )(*args)


def _cd_in_kernel(x_ref, shift_ref, scale_ref, w_ref, cos_ref, su_ref, sd_ref,
                  glu_ref, cg_ref, q_ref, k_ref, ks_ref, v_ref, vs_ref, dg_ref):
    u = (x_ref[0] * (1.0 + scale_ref[0]) + shift_ref[0]).astype(BF16)

    def cols(lo, width):
        return jnp.dot(u, w_ref[:, lo:lo + width], preferred_element_type=F32)

    glu_ref[0] = cols(0, BRANCH) * jax.nn.sigmoid(cols(BRANCH, BRANCH))
    cg_ref[0] = _silu(cols(2 * BRANCH, BRANCH)).astype(BF16)
    tabs = (cos_ref[...], su_ref[...], sd_ref[...])
    q_ref[0] = (_rope(cols(3 * BRANCH, BRANCH), *tabs) * Q_SCALE).astype(BF16)
    k = _rope(cols(4 * BRANCH, LANES), *tabs)
    v = cols(4 * BRANCH + LANES, LANES)
    k_ref[0] = k.astype(BF16)
    ks_ref[0] = pltpu.roll(k, HEAD_DIM, 1).astype(BF16)
    v_ref[0] = v.astype(BF16)
    vs_ref[0] = pltpu.roll(v, HEAD_DIM, 1).astype(BF16)
    dg_ref[0] = _silu(cols(4 * BRANCH + 2 * LANES, BRANCH)).astype(BF16)


def _cd_in(h, mod, mod_row, w_in, tables, tm):
    bsz, n, _ = h.shape
    in_specs = [pl.BlockSpec((1, tm, D_MODEL), lambda b, i: (b, i, 0))] + _mod_specs(mod_row, (0, 1))
    in_specs += [pl.BlockSpec(w_in.shape, lambda b, i: (0, 0))]
    in_specs += [pl.BlockSpec((tm, LANES), lambda b, i: (i, 0))] * 3
    wide = pl.BlockSpec((1, tm, BRANCH), lambda b, i: (b, i, 0))
    narrow = pl.BlockSpec((1, tm, LANES), lambda b, i: (b, i, 0))
    wide_bf = jax.ShapeDtypeStruct((bsz, n, BRANCH), BF16)
    narrow_bf = jax.ShapeDtypeStruct((bsz, n, LANES), BF16)
    return pl.pallas_call(
        _cd_in_kernel,
        grid=(bsz, n // tm),
        in_specs=in_specs,
        out_specs=[wide, wide, wide, narrow, narrow, narrow, narrow, wide],
        out_shape=[jax.ShapeDtypeStruct((bsz, n, BRANCH), F32), wide_bf, wide_bf,
                   narrow_bf, narrow_bf, narrow_bf, narrow_bf, wide_bf],
        compiler_params=_params(2),
        name="cd_in",
    )(h, mod, mod, w_in, *tables)


def _cd_ctx_kv_kernel(x_ref, shift_ref, scale_ref, w_ref, k_ref, ks_ref, v_ref, vs_ref):
    u = (x_ref[0] * (1.0 + scale_ref[0]) + shift_ref[0]).astype(BF16)
    kv = jnp.dot(u, w_ref[...], preferred_element_type=F32)
    k = kv[:, :LANES]
    v = kv[:, LANES:]
    k_ref[0] = k.astype(BF16)
    ks_ref[0] = pltpu.roll(k, HEAD_DIM, 1).astype(BF16)
    v_ref[0] = v.astype(BF16)
    vs_ref[0] = pltpu.roll(v, HEAD_DIM, 1).astype(BF16)


def _cd_ctx_kv(h, mod, mod_row, w_kv):
    bsz, n, _ = h.shape
    in_specs = [pl.BlockSpec((1, n, D_MODEL), lambda b: (b, 0, 0))]
    in_specs += [pl.BlockSpec((1, 1, D_MODEL), functools.partial(lambda b, p: (mod_row(b), 0, p), p=p))
                 for p in (0, 1)]
    in_specs += [pl.BlockSpec(w_kv.shape, lambda b: (0, 0))]
    spec = pl.BlockSpec((1, n, LANES), lambda b: (b, 0, 0))
    shape = jax.ShapeDtypeStruct((bsz, n, LANES), BF16)
    return pl.pallas_call(
        _cd_ctx_kv_kernel,
        grid=(bsz,),
        in_specs=in_specs,
        out_specs=[spec] * 4,
        out_shape=[shape] * 4,
        compiler_params=_params(1),
        name="cd_ctx_kv",
    )(h, mod, mod, w_kv)


def _swa_kernel(sink_ref, q_ref, dg_ref,
                kp_ref, kc_ref, kn_ref, ksp_ref, ksc_ref, ksn_ref,
                vp_ref, vc_ref, vn_ref, vsp_ref, vsc_ref, vsn_ref,
                ck_ref, cks_ref, cv_ref, cvs_ref, o_ref):
    i = pl.program_id(1)
    nsteps = pl.num_programs(1)
    n_slab = BRANCH // LANES
    stack = lambda a: jnp.concatenate([a] * n_slab, axis=0)
    r = lax.broadcasted_iota(jnp.int32, (CHUNK, CHUNK), 0)
    c = lax.broadcasted_iota(jnp.int32, (CHUNK, CHUNK), 1)
    tri_prev = stack(c >= r)
    tri_next = stack(c <= r)
    edge_prev = stack(c >= r + jnp.where(i > 0, 0, CHUNK))
    edge_next = stack(c <= r - jnp.where(i < nsteps - 1, 0, CHUNK))
    lo = _lane_mask((CHUNK, LANES), True)
    row = lax.broadcasted_iota(jnp.int32, (n_slab * CHUNK, 1), 0)
    blk = lambda t: slice(t * CHUNK, (t + 1) * CHUNK)

    layouts = (
        ((0, 2, 5, 7), (kp_ref, kc_ref, kn_ref, ck_ref), (vp_ref, vc_ref, vn_ref, cv_ref)),
        ((1, 3, 4, 6), (ksp_ref, ksc_ref, ksn_ref, cks_ref), (vsp_ref, vsc_ref, vsn_ref, cvs_ref)),
    )
    split = lambda a: [a[:, c0:c0 + LANES] for c0 in range(0, a.shape[1], LANES)]

    def band(t, refs):
        prev, cur, nxt, ctx = refs
        before = prev[0] if t == 0 else cur[0, blk(t - 1)]
        after = nxt[0] if t == SWA_BLOCKS - 1 else cur[0, blk(t + 1)]
        return before, cur[0, blk(t)], after, ctx[0]

    def scores(t, swapped):
        slabs = [q_ref[0, blk(t), j * LANES:(j + 1) * LANES] for j in range(n_slab)]
        zero = jnp.zeros_like(slabs[0])
        q_lo = [jnp.where(lo, q, zero) for q in slabs]
        q_hi = [jnp.where(lo, zero, q) for q in slabs]
        qs = jnp.concatenate((q_hi[:2] + q_lo[2:]) if swapped else (q_lo[:2] + q_hi[2:]), axis=0)
        masks = (edge_prev if t == 0 else tri_prev, None, edge_next if t == SWA_BLOCKS - 1 else tri_next, None)
        ss = [lax.dot_general(qs, kb, _NT, preferred_element_type=F32) for kb in band(t, layouts[swapped][1])]
        return [s if mk is None else jnp.where(mk, s, NEG_INF) for s, mk in zip(ss, masks)]

    def softmax(ss, swapped):
        sink = jnp.zeros((n_slab * CHUNK, 1), F32)
        for j, head in enumerate(layouts[swapped][0]):
            sink = jnp.where((row >= j * CHUNK) & (row < (j + 1) * CHUNK), sink_ref[head] * LOG2E, sink)
        m = jnp.max(functools.reduce(jnp.maximum, sum((split(s) for s in ss), [])), axis=-1, keepdims=True)
        m = jnp.maximum(m, sink)
        ps = [jnp.exp2(s - m) for s in ss]
        l = jnp.sum(functools.reduce(jnp.add, sum((split(p) for p in ps), [])), axis=-1, keepdims=True)
        l = l + jnp.exp2(sink - m)
        return [p.astype(BF16) for p in ps], 1.0 / l

    def weighted_values(ps, inv_l, t, swapped):
        acc = None
        for p, vb in zip(ps, band(t, layouts[swapped][2])):
            pv = jnp.dot(p, vb, preferred_element_type=F32)
            acc = pv if acc is None else acc + pv
        return acc * inv_l

    def emit(t, plain, swapped):
        out = []
        for j in range(n_slab):
            a = plain[j * CHUNK:(j + 1) * CHUNK]
            b = swapped[j * CHUNK:(j + 1) * CHUNK]
            out.append(jnp.where(lo, a, b) if j < 2 else jnp.where(lo, b, a))
        o = jnp.concatenate(out, axis=-1)
        o_ref[0, blk(t), :] = (o * dg_ref[0, blk(t), :].astype(F32)).astype(BF16)

    groups = [(t, swapped) for t in range(SWA_BLOCKS) for swapped in (0, 1)]
    ss_next = scores(*groups[0])
    normed = []
    for idx, (t, swapped) in enumerate(groups):
        ss = ss_next
        if idx + 1 < len(groups):
            ss_next = scores(*groups[idx + 1])
        ps, inv_l = softmax(ss, swapped)
        normed.append(weighted_values(ps, inv_l, t, swapped))
        if swapped:
            emit(t, normed[-2], normed[-1])


def _swa(sink, q, dg, k, ks, v, vs, ck, cks, cv, cvs):
    bsz, n, _ = q.shape
    nb = n // CHUNK
    tq = SWA_BLOCKS * CHUNK
    qspec = pl.BlockSpec((1, tq, BRANCH), lambda b, i: (b, i, 0))
    prev = pl.BlockSpec((1, CHUNK, LANES), lambda b, i: (b, jnp.maximum(SWA_BLOCKS * i - 1, 0), 0))
    cur = pl.BlockSpec((1, tq, LANES), lambda b, i: (b, i, 0))
    nxt = pl.BlockSpec((1, CHUNK, LANES), lambda b, i: (b, jnp.minimum(SWA_BLOCKS * (i + 1), nb - 1), 0))
    cspec = pl.BlockSpec((1, ck.shape[1], LANES), lambda b, i: (b, 0, 0))
    in_specs = [pl.BlockSpec(memory_space=pltpu.SMEM), qspec, qspec] + [prev, cur, nxt] * 4 + [cspec] * 4
    return pl.pallas_call(
        _swa_kernel,
        grid=(bsz, nb // SWA_BLOCKS),
        in_specs=in_specs,
        out_specs=qspec,
        out_shape=jax.ShapeDtypeStruct((bsz, n, BRANCH), BF16),
        compiler_params=_params(2),
        name="swa",
    )(sink, q, dg, k, k, k, ks, ks, ks, v, v, v, vs, vs, vs, ck, cks, cv, cvs)


def _cd_out_kernel(glu_ref, gp_ref, gn_ref, cg_ref, dm_ref, h_ref, gate_ref, dw_ref, db_ref, cng_ref, cnb_ref,
                   w_ref, lg_ref, lb_ref, o_ref, ext_ref, y_ref, *, alpha):
    i = pl.program_id(1)
    nb = pl.num_programs(1)
    tm = glu_ref.shape[1]
    ext_ref[0:CONV_HALO, :] = jnp.where(i > 0, gp_ref[0], 0.0)
    ext_ref[CONV_HALO:CONV_HALO + tm, :] = glu_ref[0]
    ext_ref[CONV_HALO + tm:, :] = jnp.where(i < nb - 1, gn_ref[0], 0.0)
    off = CONV_HALO - C_KERNEL // 2
    span = CONV_ROWS + 8

    def conv_rows(t, carry):
        r0 = pl.multiple_of(t * CONV_ROWS, CONV_ROWS)
        for c0 in range(0, BRANCH, LANES):
            win = ext_ref[pl.ds(r0, CONV_ROWS + 2 * CONV_HALO), c0:c0 + LANES]
            y = None
            for s in range(8):
                z = None
                for a in range((off + C_KERNEL + 7) // 8):
                    j = 8 * a + s - off
                    if 0 <= j < C_KERNEL:
                        term = win[8 * a:8 * a + span] * dw_ref[j:j + 1, c0:c0 + LANES]
                        z = term if z is None else z + term
                zs = z[s:s + CONV_ROWS]
                y = zs if y is None else y + zs
            y_ref[pl.ds(r0, CONV_ROWS), c0:c0 + LANES] = y
        return carry

    lax.fori_loop(0, tm // CONV_ROWS, conv_rows, 0)
    y = y_ref[...] + db_ref[...]
    c_lat = _silu(_layer_norm(y, cng_ref[...], cnb_ref[...]))
    c_mix = (c_lat * cg_ref[0].astype(F32)).astype(BF16)
    o_ref[0] = _out_proj_ln(c_mix, dm_ref[0], w_ref, h_ref[0], gate_ref[0], lg_ref[...], lb_ref[...], alpha)


def _cd_out(glu, cg, dm, h, mod, mod_row, dw_w, dw_b, cng, cnb, w_out, lg, lb, alpha, tm):
    bsz, n, _ = h.shape
    per = tm // CONV_HALO
    last = n // CONV_HALO - 1
    bspec = pl.BlockSpec((1, tm, BRANCH), lambda b, i: (b, i, 0))
    hspec = pl.BlockSpec((1, tm, D_MODEL), lambda b, i: (b, i, 0))
    vec = lambda width: pl.BlockSpec((1, width), lambda b, i: (0, 0))
    in_specs = [
        bspec,
        pl.BlockSpec((1, CONV_HALO, BRANCH), lambda b, i: (b, jnp.maximum(i * per - 1, 0), 0)),
        pl.BlockSpec((1, CONV_HALO, BRANCH), lambda b, i: (b, jnp.minimum((i + 1) * per, last), 0)),
        bspec, bspec, hspec,
    ] + _mod_specs(mod_row, (2,)) + [
        pl.BlockSpec(dw_w.shape, lambda b, i: (0, 0)),
        vec(BRANCH), vec(BRANCH), vec(BRANCH),
        pl.BlockSpec(w_out.shape, lambda b, i: (0, 0)),
        vec(D_MODEL), vec(D_MODEL),
    ]
    return pl.pallas_call(
        functools.partial(_cd_out_kernel, alpha=alpha),
        grid=(bsz, n // tm),
        in_specs=in_specs,
        out_specs=hspec,
        out_shape=jax.ShapeDtypeStruct((bsz, n, D_MODEL), F32),
        scratch_shapes=[pltpu.VMEM((tm + 2 * CONV_HALO, BRANCH), F32), pltpu.VMEM((tm, BRANCH), F32)],
        compiler_params=_params(2),
        name="cd_out",
    )(glu, glu, glu, cg, dm, h, mod, dw_w, dw_b, cng, cnb, w_out, lg, lb)


def kernel(x, c, ctx, c_ctx, mod_w, mod_b, ln_g, ln_b, ab_w_in, ab_w_out, a_w_s, a_b_s, a_norm_g, a_norm_b,
           b_lq1, b_lk1, b_lq2, b_lk2, b_subln_g, cd_w_in, cd_w_out, c_dw_w, c_dw_b, c_norm_g, c_norm_b, d_sink):
    bsz, n, _ = x.shape
    assert DEPTH == 2 and bsz + 1 <= MOD_ROWS
    alpha = (2.0 * DEPTH) ** 0.25
    tables = _rope_tables(n)

    cc = jnp.concatenate([c, c_ctx[None, :], jnp.zeros((MOD_ROWS - bsz - 1, D_MODEL), F32)], axis=0)
    mod = _modulation(cc, mod_w, mod_b).reshape(DEPTH * MOD_ROWS, 1, 3 * D_MODEL)
    lat_row = lambda layer: (lambda b: layer * MOD_ROWS + b)
    ctx_row = lambda layer: (lambda b: layer * MOD_ROWS + bsz)
    row2d = lambda a: a.reshape(1, -1)

    w_in0 = ab_w_in[0].astype(BF16)
    w_out0 = ab_w_out[0].astype(BF16)
    ws = a_w_s[0].astype(BF16)
    bs2d = jnp.repeat(a_b_s[0].T, LANES, axis=1)
    ng, nbias = row2d(a_norm_g[0]), row2d(a_norm_b[0])
    lvec = jnp.stack([b_lq1[0], b_lk1[0], b_lq2[0], b_lk2[0]])
    subln = row2d(b_subln_g[0])
    lam_init0 = 0.8 - 0.6 * math.exp(-0.3 * 0)
    lg0, lb0 = row2d(ln_g[0]), row2d(ln_b[0])

    ctx_len = ctx.shape[1]
    assert (bsz * ctx_len) % ROW_TILE == 0 and ROW_TILE % ctx_len == 0
    tiled = lambda a: a.reshape(-1, ROW_TILE, a.shape[-1])
    per_batch = lambda a: a.reshape(bsz, ctx_len, a.shape[-1])

    ta, vn, q, k, v, sg = _ab_in(x, mod, lat_row(0), w_in0, ng, nbias, tables, tm=ROW_TILE)
    cta, cvn, cq, ck, cv, csg = _ab_in(tiled(ctx), mod, ctx_row(0), w_in0, ng, nbias, None, tm=ROW_TILE)
    cta, cvn, cq, ck, cv, csg = map(per_batch, (cta, cvn, cq, ck, cv, csg))
    h1 = _ab_attn_out(lvec, subln, q, sg, ta, vn, x, mod, lat_row(0), ws, bs2d, w_out0, lg0, lb0,
                      [(ck, cv), (k, v)], lam_init0, alpha, tq=ATTN_TILE)
    hc1 = _ab_attn_out(lvec, subln, cq, csg, cta, cvn, ctx, mod, ctx_row(0), ws, bs2d, w_out0, lg0, lb0,
                       [(ck, cv)], lam_init0, alpha, tq=ctx_len)

    w_in1 = cd_w_in[0].astype(BF16)
    w_out1 = cd_w_out[0].astype(BF16)
    kv_lo = 4 * BRANCH
    glu, cg, dq, dk, dks, dv, dvs, dg = _cd_in(h1, mod, lat_row(1), w_in1, tables, tm=ROW_TILE)
    ck1, cks1, cv1, cvs1 = map(per_batch, _cd_ctx_kv(tiled(hc1), mod, ctx_row(1), w_in1[:, kv_lo:kv_lo + 2 * LANES]))
    dm = _swa(d_sink[0], dq, dg, dk, dks, dv, dvs, ck1, cks1, cv1, cvs1)
    return _cd_out(glu, cg, dm, h1, mod, lat_row(1), c_dw_w[0], row2d(c_dw_b[0]), row2d(c_norm_g[0]),
                   row2d(c_norm_b[0]), w_out1, row2d(ln_g[1]), row2d(ln_b[1]), alpha, tm=ROW_TILE)
```

```python
import functools
import math

import jax
import jax.numpy as jnp
import numpy as np
from jax import lax
from jax.experimental import pallas as pl
from jax.experimental.pallas import tpu as pltpu

F32 = jnp.float32
BF16 = jnp.bfloat16

D_MODEL = 1024
DEPTH = 2
CTX_LEN = 256
GRID_W = 64
ROPE_THETA = 10000.0
LN_EPS = 1e-6
RMS_EPS = 1e-5
NEG_INF = -1e30
HEAD_DIM = 64
BRANCH = 512
CHUNK = 128
C_KERNEL = 31
CONV_HALO = 16
MOD_ROWS = 24
LANES = 128

VMEM_LIMIT = 56 * 1024 * 1024
LOG2E = math.log2(math.e)
Q_SCALE = HEAD_DIM ** -0.5 * LOG2E
KEY_CHUNK = 256
CONV_ROWS = 128
ROW_TILE = 1024
ATTN_TILE = 512
ATTN_SUB = 256
SWA_BLOCKS = 8

_NT = (((1,), (1,)), ((), ()))


def _params(n_axes, vmem=VMEM_LIMIT):
    return pltpu.CompilerParams(dimension_semantics=("arbitrary",) * n_axes, vmem_limit_bytes=vmem)


def _layer_norm(x, g, b):
    mu = jnp.mean(x, axis=-1, keepdims=True)
    xc = x - mu
    var = jnp.mean(xc * xc, axis=-1, keepdims=True)
    return xc * lax.rsqrt(var + LN_EPS) * g + b


def _lane_mask(shape, lo):
    lane = lax.broadcasted_iota(jnp.int32, shape, len(shape) - 1) % LANES
    return (lane < HEAD_DIM) if lo else (lane >= HEAD_DIM)


def _mod_kernel(c_ref, w_ref, b_ref, o_ref):
    c = c_ref[...]
    a = c * jax.nn.sigmoid(c)
    a_hi = a.astype(BF16)
    a_lo = (a - a_hi.astype(F32)).astype(BF16)
    w = w_ref[0]
    w_hi = w.astype(BF16)
    w_lo = (w - w_hi.astype(F32)).astype(BF16)
    acc = jnp.dot(a_hi, w_hi, preferred_element_type=F32)
    acc = acc + jnp.dot(a_hi, w_lo, preferred_element_type=F32)
    acc = acc + jnp.dot(a_lo, w_hi, preferred_element_type=F32)
    o_ref[0] = acc + b_ref[0]


def _modulation(cc, mod_w, mod_b):
    tn = 768
    n_out = 3 * D_MODEL
    return pl.pallas_call(
        _mod_kernel,
        grid=(DEPTH, n_out // tn),
        in_specs=[
            pl.BlockSpec((MOD_ROWS, D_MODEL), lambda l, j: (0, 0)),
            pl.BlockSpec((1, D_MODEL, tn), lambda l, j: (l, 0, j)),
            pl.BlockSpec((1, 1, tn), lambda l, j: (l, 0, j)),
        ],
        out_specs=pl.BlockSpec((1, MOD_ROWS, tn), lambda l, j: (l, 0, j)),
        out_shape=jax.ShapeDtypeStruct((DEPTH, MOD_ROWS, n_out), F32),
        compiler_params=_params(2),
        name="modulation",
    )(cc, mod_w, mod_b.reshape(DEPTH, 1, n_out))


def _mod_specs(row_fn, parts):
    return [pl.BlockSpec((1, 1, D_MODEL), functools.partial(lambda b, i, p: (row_fn(b), 0, p), p=p))
            for p in parts]


def _rope_tables(n):
    m = HEAD_DIM // 4
    inv = ROPE_THETA ** (-np.arange(m, dtype=np.float64) / m)
    t = np.arange(n)
    ang_r = (t // GRID_W)[:, None] * inv
    ang_c = (t % GRID_W)[:, None] * inv
    z = np.zeros_like(ang_r)
    cos = np.concatenate([np.cos(ang_r)] * 2 + [np.cos(ang_c)] * 2, axis=-1)
    sin_up = np.concatenate([-np.sin(ang_r), z, -np.sin(ang_c), z], axis=-1)
    sin_dn = np.concatenate([z, np.sin(ang_r), z, np.sin(ang_c)], axis=-1)
    tile = lambda a: jnp.asarray(np.concatenate([a, a], axis=-1), dtype=F32)
    return tile(cos), tile(sin_up), tile(sin_dn)


def _rope(x, cos, sin_up, sin_dn):
    outs = []
    for s in range(x.shape[-1] // LANES):
        xs = x[:, s * LANES:(s + 1) * LANES]
        up = pltpu.roll(xs, LANES - 16, 1)
        dn = pltpu.roll(xs, 16, 1)
        outs.append(xs * cos + up * sin_up + dn * sin_dn)
    return outs[0] if len(outs) == 1 else jnp.concatenate(outs, axis=-1)


def _silu(x):
    return x * jax.nn.sigmoid(x)


def _ab_in_kernel(*refs, rope):
    if rope:
        (x_ref, shift_ref, scale_ref, w_ref, ng_ref, nb_ref, cos_ref, su_ref, sd_ref,
         ta_ref, vn_ref, q_ref, k_ref, v_ref, sg_ref) = refs
    else:
        (x_ref, shift_ref, scale_ref, w_ref, ng_ref, nb_ref,
         ta_ref, vn_ref, q_ref, k_ref, v_ref, sg_ref) = refs
    u = (x_ref[0] * (1.0 + scale_ref[0]) + shift_ref[0]).astype(BF16)

    def seg(s):
        return jnp.dot(u, w_ref[:, s * BRANCH:(s + 1) * BRANCH], preferred_element_type=F32)

    ta_ref[0] = (jax.nn.gelu(seg(0)) * _silu(seg(2))).astype(BF16)
    vn_ref[0] = _layer_norm(jax.nn.gelu(seg(1)), ng_ref[...], nb_ref[...]).astype(BF16)
    q = seg(3)
    k = seg(4)
    if rope:
        tabs = (cos_ref[...], su_ref[...], sd_ref[...])
        q = _rope(q, *tabs)
        k = _rope(k, *tabs)
    q_ref[0] = (q * Q_SCALE).astype(BF16)
    k_ref[0] = k.astype(BF16)
    v_ref[0] = seg(5).astype(BF16)
    sg_ref[0] = _silu(seg(6)).astype(BF16)


def _ab_in(h, mod, mod_row, w_in, ng, nb, tables, tm):
    bsz, n, _ = h.shape
    rope = tables is not None
    in_specs = [pl.BlockSpec((1, tm, D_MODEL), lambda b, i: (b, i, 0))]
    in_specs += _mod_specs(mod_row, (0, 1))
    in_specs += [
        pl.BlockSpec(w_in.shape, lambda b, i: (0, 0)),
        pl.BlockSpec((1, BRANCH), lambda b, i: (0, 0)),
        pl.BlockSpec((1, BRANCH), lambda b, i: (0, 0)),
    ]
    args = [h, mod, mod, w_in, ng, nb]
    if rope:
        in_specs += [pl.BlockSpec((tm, LANES), lambda b, i: (i, 0))] * 3
        args += list(tables)
    out_spec = pl.BlockSpec((1, tm, BRANCH), lambda b, i: (b, i, 0))
    out_shape = jax.ShapeDtypeStruct((bsz, n, BRANCH), BF16)
    return pl.pallas_call(
        functools.partial(_ab_in_kernel, rope=rope),
        grid=(bsz, n // tm),
        in_specs=in_specs,
        out_specs=[out_spec] * 6,
        out_shape=[out_shape] * 6,
        compiler_params=_params(2),
        name="ab_in_rope" if rope else "ab_in_ctx",
    )(*args)


def _out_proj_ln(left, right, w_ref, h, gate, lg, lb, alpha):
    y = jnp.dot(left, w_ref[:BRANCH, :], preferred_element_type=F32)
    y = y + jnp.dot(right, w_ref[BRANCH:, :], preferred_element_type=F32)
    return _layer_norm(alpha * h + gate * y, lg, lb)


def _ab_attn_out_kernel(*refs, seg_rows, lam_init, alpha):
    n_seg = len(seg_rows)
    (lv_ref, g_ref, q_ref, sg_ref, ta_ref, vn_ref, h_ref, gate_ref, ws_ref, bs_ref, w_ref, lg_ref,
     lb_ref) = refs[:13]
    kv = refs[13:13 + 2 * n_seg]
    o_ref = refs[13 + 2 * n_seg]
    vext_ref = refs[14 + 2 * n_seg]
    bm_ref = refs[15 + 2 * n_seg]
    heads = BRANCH // LANES

    @pl.when(pl.program_id(1) == 0)
    def _fill_values():
        for h in range(heads):
            base = 0
            for j, rows in enumerate(seg_rows):
                vext_ref[h, base:base + rows, :LANES] = kv[2 * j + 1][0, :, h * LANES:(h + 1) * LANES]
                base += rows
            vext_ref[h, :, LANES:] = jnp.ones((base, LANES), BF16)

    lv = lv_ref[...]
    lam = (jnp.exp(jnp.sum(lv[0:1] * lv[1:2], axis=-1, keepdims=True))
           - jnp.exp(jnp.sum(lv[2:3] * lv[3:4], axis=-1, keepdims=True)) + lam_init)
    tq = min(q_ref.shape[1], ATTN_SUB)
    lo = _lane_mask((tq, LANES), True)

    def attend(r0, h):
        cols = slice(h * LANES, (h + 1) * LANES)
        q = q_ref[0, r0:r0 + tq, cols]
        zero = jnp.zeros_like(q)
        qs = jnp.concatenate([jnp.where(lo, q, zero), jnp.where(lo, zero, q)], axis=0)
        m = acc = None
        base = 0
        for j, rows in enumerate(seg_rows):
            for c0 in range(0, rows, KEY_CHUNK):
                n = min(KEY_CHUNK, rows - c0)
                s = lax.dot_general(qs, kv[2 * j][0, c0:c0 + n, cols], _NT, preferred_element_type=F32)
                smax = functools.reduce(jnp.maximum, [s[:, l0:l0 + LANES] for l0 in range(0, n, LANES)])
                smax = jnp.max(smax, axis=-1, keepdims=True)
                m_new = smax if m is None else jnp.maximum(m, smax)
                p = jnp.exp2(s - m_new).astype(BF16)
                pv = jnp.dot(p, vext_ref[h, base + c0:base + c0 + n, :], preferred_element_type=F32)
                acc = pv if acc is None else jnp.exp2(m - m_new) * acc + pv
                m = m_new
            base += rows
        on = acc[:, :LANES] * (1.0 / acc[:, LANES:])
        o = on[:tq] - lam * on[tq:]
        o = o * lax.rsqrt(jnp.mean(o * o, axis=-1, keepdims=True) + RMS_EPS) * g_ref[...] * (1.0 - lam_init)
        bm_ref[r0:r0 + tq, cols] = (o * sg_ref[0, r0:r0 + tq, cols].astype(F32)).astype(BF16)

    def mix_and_project(r0):
        rows = []
        for c0 in range(r0, r0 + tq, CHUNK):
            cols = [jnp.dot(ws_ref[g], vn_ref[0, c0:c0 + CHUNK, g * LANES:(g + 1) * LANES],
                            preferred_element_type=F32) for g in range(heads)]
            rows.append(jnp.concatenate(cols, axis=-1) + bs_ref[...])
        mixed = rows[0] if len(rows) == 1 else jnp.concatenate(rows, axis=0)
        a_mix = (ta_ref[0, r0:r0 + tq, :].astype(F32) * mixed).astype(BF16)
        o_ref[0, r0:r0 + tq, :] = _out_proj_ln(a_mix, bm_ref[r0:r0 + tq, :], w_ref, h_ref[0, r0:r0 + tq, :],
                                               gate_ref[0], lg_ref[...], lb_ref[...], alpha)

    starts = list(range(0, q_ref.shape[1], tq))
    for idx, r0 in enumerate(starts):
        for h in range(heads):
            attend(r0, h)
            if h == 0 and idx > 0:
                mix_and_project(starts[idx - 1])
    mix_and_project(starts[-1])


def _ab_attn_out(lvec, subln_g, q, sg, ta, vn, h, mod, mod_row, ws, bs2d, w_out, lg, lb, segs, lam_init, alpha, tq):
    bsz, n, _ = q.shape
    qspec = pl.BlockSpec((1, tq, BRANCH), lambda b, i: (b, i, 0))
    hspec = pl.BlockSpec((1, tq, D_MODEL), lambda b, i: (b, i, 0))
    whole = lambda a: pl.BlockSpec(a.shape, lambda b, i: (0,) * a.ndim)
    in_specs = [whole(lvec), whole(subln_g), qspec, qspec, qspec, qspec, hspec] + _mod_specs(mod_row, (2,)) + [
        whole(ws), whole(bs2d), whole(w_out), whole(lg), whole(lb)]
    args = [lvec, subln_g, q, sg, ta, vn, h, mod, ws, bs2d, w_out, lg, lb]
    for k, v in segs:
        spec = pl.BlockSpec((1, k.shape[1], BRANCH), lambda b, i: (b, 0, 0))
        in_specs += [spec, spec]
        args += [k, v]
    seg_rows = tuple(k.shape[1] for k, _ in segs)
    return pl.pallas_call(
        functools.partial(_ab_attn_out_kernel, seg_rows=seg_rows, lam_init=lam_init, alpha=alpha),
        grid=(bsz, n // tq),
        in_specs=in_specs,
        out_specs=hspec,
        out_shape=jax.ShapeDtypeStruct((bsz, n, D_MODEL), F32),
        scratch_shapes=[pltpu.VMEM((BRANCH // LANES, sum(seg_rows), 2 * LANES), BF16),
                        pltpu.VMEM((tq, BRANCH), BF16)],
        compiler_params=_params(2),
        name=f"ab_attn_out_{len(segs)}seg",
    )(*args)


def _cd_in_kernel(x_ref, shift_ref, scale_ref, w_ref, cos_ref, su_ref, sd_ref,
                  glu_ref, cg_ref, q_ref, k_ref, ks_ref, v_ref, vs_ref, dg_ref):
    u = (x_ref[0] * (1.0 + scale_ref[0]) + shift_ref[0]).astype(BF16)

    def cols(lo, width):
        return jnp.dot(u, w_ref[:, lo:lo + width], preferred_element_type=F32)

    glu_ref[0] = cols(0, BRANCH) * jax.nn.sigmoid(cols(BRANCH, BRANCH))
    cg_ref[0] = _silu(cols(2 * BRANCH, BRANCH)).astype(BF16)
    tabs = (cos_ref[...], su_ref[...], sd_ref[...])
    q_ref[0] = (_rope(cols(3 * BRANCH, BRANCH), *tabs) * Q_SCALE).astype(BF16)
    k = _rope(cols(4 * BRANCH, LANES), *tabs)
    v = cols(4 * BRANCH + LANES, LANES)
    k_ref[0] = k.astype(BF16)
    ks_ref[0] = pltpu.roll(k, HEAD_DIM, 1).astype(BF16)
    v_ref[0] = v.astype(BF16)
    vs_ref[0] = pltpu.roll(v, HEAD_DIM, 1).astype(BF16)
    dg_ref[0] = _silu(cols(4 * BRANCH + 2 * LANES, BRANCH)).astype(BF16)


def _cd_in(h, mod, mod_row, w_in, tables, tm):
    bsz, n, _ = h.shape
    in_specs = [pl.BlockSpec((1, tm, D_MODEL), lambda b, i: (b, i, 0))] + _mod_specs(mod_row, (0, 1))
    in_specs += [pl.BlockSpec(w_in.shape, lambda b, i: (0, 0))]
    in_specs += [pl.BlockSpec((tm, LANES), lambda b, i: (i, 0))] * 3
    wide = pl.BlockSpec((1, tm, BRANCH), lambda b, i: (b, i, 0))
    narrow = pl.BlockSpec((1, tm, LANES), lambda b, i: (b, i, 0))
    wide_bf = jax.ShapeDtypeStruct((bsz, n, BRANCH), BF16)
    narrow_bf = jax.ShapeDtypeStruct((bsz, n, LANES), BF16)
    return pl.pallas_call(
        _cd_in_kernel,
        grid=(bsz, n // tm),
        in_specs=in_specs,
        out_specs=[wide, wide, wide, narrow, narrow, narrow, narrow, wide],
        out_shape=[jax.ShapeDtypeStruct((bsz, n, BRANCH), F32), wide_bf, wide_bf,
                   narrow_bf, narrow_bf, narrow_bf, narrow_bf, wide_bf],
        compiler_params=_params(2),
        name="cd_in",
    )(h, mod, mod, w_in, *tables)


def _cd_ctx_kv_kernel(x_ref, shift_ref, scale_ref, w_ref, k_ref, ks_ref, v_ref, vs_ref):
    u = (x_ref[0] * (1.0 + scale_ref[0]) + shift_ref[0]).astype(BF16)
    kv = jnp.dot(u, w_ref[...], preferred_element_type=F32)
    k = kv[:, :LANES]
    v = kv[:, LANES:]
    k_ref[0] = k.astype(BF16)
    ks_ref[0] = pltpu.roll(k, HEAD_DIM, 1).astype(BF16)
    v_ref[0] = v.astype(BF16)
    vs_ref[0] = pltpu.roll(v, HEAD_DIM, 1).astype(BF16)


def _cd_ctx_kv(h, mod, mod_row, w_kv):
    bsz, n, _ = h.shape
    in_specs = [pl.BlockSpec((1, n, D_MODEL), lambda b: (b, 0, 0))]
    in_specs += [pl.BlockSpec((1, 1, D_MODEL), functools.partial(lambda b, p: (mod_row(b), 0, p), p=p))
                 for p in (0, 1)]
    in_specs += [pl.BlockSpec(w_kv.shape, lambda b: (0, 0))]
    spec = pl.BlockSpec((1, n, LANES), lambda b: (b, 0, 0))
    shape = jax.ShapeDtypeStruct((bsz, n, LANES), BF16)
    return pl.pallas_call(
        _cd_ctx_kv_kernel,
        grid=(bsz,),
        in_specs=in_specs,
        out_specs=[spec] * 4,
        out_shape=[shape] * 4,
        compiler_params=_params(1),
        name="cd_ctx_kv",
    )(h, mod, mod, w_kv)


def _conv_piece(win, dw_ref, c0):
    rows = win.shape[0] - 3 * CONV_HALO
    off = CONV_HALO - C_KERNEL // 2
    span = rows + 16
    even = win[0:rows + 32].astype(BF16)
    odd = win[8:rows + 40].astype(BF16)
    y = None
    for s in range(8):
        z = None
        for a in range((off + C_KERNEL + 7) // 8):
            j = 8 * a + s - off
            if 0 <= j < C_KERNEL:
                src = odd if a % 2 else even
                base = 16 * (a // 2)
                w = pltpu.repeat(dw_ref[16 * j:16 * j + 16, c0:c0 + LANES], span // 16, axis=0)
                term = src[base:base + span] * w
                z = term if z is None else z + term
        zs = z.astype(F32)[s:s + rows]
        y = zs if y is None else y + zs
    return y


def _fill_conv_window(ext_ref, glu_ref, gp_ref, gn_ref, first, last):
    tm = glu_ref.shape[1]
    ext_ref[0:CONV_HALO, :] = jnp.where(first, 0.0, gp_ref[0])
    ext_ref[CONV_HALO:CONV_HALO + tm, :] = glu_ref[0]
    ext_ref[CONV_HALO + tm:2 * CONV_HALO + tm, :] = jnp.where(last, 0.0, gn_ref[0])
    ext_ref[2 * CONV_HALO + tm:, :] = jnp.zeros((CONV_HALO, ext_ref.shape[1]), F32)


def _conv_specs(tm, n, width, lane_block):
    per = tm // CONV_HALO
    last = n // CONV_HALO - 1
    return [
        pl.BlockSpec((1, tm, width), lambda b, i: (b, i, lane_block)),
        pl.BlockSpec((1, CONV_HALO, width), lambda b, i: (b, jnp.maximum(i * per - 1, 0), lane_block)),
        pl.BlockSpec((1, CONV_HALO, width), lambda b, i: (b, jnp.minimum((i + 1) * per, last), lane_block)),
    ]


def _swa_kernel(sink_ref, q_ref, dg_ref,
                kp_ref, kc_ref, kn_ref, ksp_ref, ksc_ref, ksn_ref,
                vp_ref, vc_ref, vn_ref, vsp_ref, vsc_ref, vsn_ref,
                ck_ref, cks_ref, cv_ref, cvs_ref, o_ref):
    i = pl.program_id(1)
    nsteps = pl.num_programs(1)
    n_slab = BRANCH // LANES
    stack = lambda a: jnp.concatenate([a] * n_slab, axis=0)
    r = lax.broadcasted_iota(jnp.int32, (CHUNK, CHUNK), 0)
    c = lax.broadcasted_iota(jnp.int32, (CHUNK, CHUNK), 1)
    tri_prev = stack(c >= r)
    tri_next = stack(c <= r)
    edge_prev = stack(c >= r + jnp.where(i > 0, 0, CHUNK))
    edge_next = stack(c <= r - jnp.where(i < nsteps - 1, 0, CHUNK))
    lo = _lane_mask((CHUNK, LANES), True)
    row = lax.broadcasted_iota(jnp.int32, (n_slab * CHUNK, 1), 0)
    blk = lambda t: slice(t * CHUNK, (t + 1) * CHUNK)

    layouts = (
        ((0, 2, 5, 7), (kp_ref, kc_ref, kn_ref, ck_ref), (vp_ref, vc_ref, vn_ref, cv_ref)),
        ((1, 3, 4, 6), (ksp_ref, ksc_ref, ksn_ref, cks_ref), (vsp_ref, vsc_ref, vsn_ref, cvs_ref)),
    )
    split = lambda a: [a[:, c0:c0 + LANES] for c0 in range(0, a.shape[1], LANES)]

    def band(t, refs):
        prev, cur, nxt, ctx = refs
        before = prev[0] if t == 0 else cur[0, blk(t - 1)]
        after = nxt[0] if t == SWA_BLOCKS - 1 else cur[0, blk(t + 1)]
        return before, cur[0, blk(t)], after, ctx[0]

    def scores(t, swapped):
        slabs = [q_ref[0, blk(t), j * LANES:(j + 1) * LANES] for j in range(n_slab)]
        zero = jnp.zeros_like(slabs[0])
        q_lo = [jnp.where(lo, q, zero) for q in slabs]
        q_hi = [jnp.where(lo, zero, q) for q in slabs]
        qs = jnp.concatenate((q_hi[:2] + q_lo[2:]) if swapped else (q_lo[:2] + q_hi[2:]), axis=0)
        masks = (edge_prev if t == 0 else tri_prev, None, edge_next if t == SWA_BLOCKS - 1 else tri_next, None)
        ss = [lax.dot_general(qs, kb, _NT, preferred_element_type=F32) for kb in band(t, layouts[swapped][1])]
        return [s if mk is None else jnp.where(mk, s, NEG_INF) for s, mk in zip(ss, masks)]

    def softmax(ss, swapped):
        sink = jnp.zeros((n_slab * CHUNK, 1), F32)
        for j, head in enumerate(layouts[swapped][0]):
            sink = jnp.where((row >= j * CHUNK) & (row < (j + 1) * CHUNK), sink_ref[head] * LOG2E, sink)
        m = jnp.max(functools.reduce(jnp.maximum, sum((split(s) for s in ss), [])), axis=-1, keepdims=True)
        m = jnp.maximum(m, sink)
        ps = [jnp.exp2(s - m) for s in ss]
        l = jnp.sum(functools.reduce(jnp.add, sum((split(p) for p in ps), [])), axis=-1, keepdims=True)
        l = l + jnp.exp2(sink - m)
        return [p.astype(BF16) for p in ps], 1.0 / l

    def weighted_values(ps, inv_l, t, swapped):
        acc = None
        for p, vb in zip(ps, band(t, layouts[swapped][2])):
            pv = jnp.dot(p, vb, preferred_element_type=F32)
            acc = pv if acc is None else acc + pv
        return acc * inv_l

    def emit(t, plain, swapped):
        out = []
        for j in range(n_slab):
            a = plain[j * CHUNK:(j + 1) * CHUNK]
            b = swapped[j * CHUNK:(j + 1) * CHUNK]
            out.append(jnp.where(lo, a, b) if j < 2 else jnp.where(lo, b, a))
        o = jnp.concatenate(out, axis=-1)
        o_ref[0, blk(t), :] = (o * dg_ref[0, blk(t), :].astype(F32)).astype(BF16)

    groups = [(t, swapped) for t in range(SWA_BLOCKS) for swapped in (0, 1)]
    ss_next = scores(*groups[0])
    normed = []
    for idx, (t, swapped) in enumerate(groups):
        ss = ss_next
        if idx + 1 < len(groups):
            ss_next = scores(*groups[idx + 1])
        ps, inv_l = softmax(ss, swapped)
        normed.append(weighted_values(ps, inv_l, t, swapped))
        if swapped:
            emit(t, normed[-2], normed[-1])


def _swa(sink, q, dg, k, ks, v, vs, ck, cks, cv, cvs):
    bsz, n, _ = q.shape
    nb = n // CHUNK
    tq = SWA_BLOCKS * CHUNK
    qspec = pl.BlockSpec((1, tq, BRANCH), lambda b, i: (b, i, 0))
    prev = pl.BlockSpec((1, CHUNK, LANES), lambda b, i: (b, jnp.maximum(SWA_BLOCKS * i - 1, 0), 0))
    cur = pl.BlockSpec((1, tq, LANES), lambda b, i: (b, i, 0))
    nxt = pl.BlockSpec((1, CHUNK, LANES), lambda b, i: (b, jnp.minimum(SWA_BLOCKS * (i + 1), nb - 1), 0))
    cspec = pl.BlockSpec((1, ck.shape[1], LANES), lambda b, i: (b, 0, 0))
    in_specs = [pl.BlockSpec(memory_space=pltpu.SMEM), qspec, qspec] + [prev, cur, nxt] * 4 + [cspec] * 4
    return pl.pallas_call(
        _swa_kernel,
        grid=(bsz, nb // SWA_BLOCKS),
        in_specs=in_specs,
        out_specs=qspec,
        out_shape=jax.ShapeDtypeStruct((bsz, n, BRANCH), BF16),
        compiler_params=_params(2),
        name="swa",
    )(sink, q, dg, k, k, k, ks, ks, ks, v, v, v, vs, vs, vs, ck, cks, cv, cvs)


def _cd_out_kernel(glu_ref, gp_ref, gn_ref, cg_ref, dm_ref, h_ref, gate_ref, dw_ref, db_ref, cng_ref, cnb_ref,
                   w_ref, lg_ref, lb_ref, o_ref, ext_ref, y_ref, *, alpha):
    i = pl.program_id(1)
    tm = glu_ref.shape[1]
    _fill_conv_window(ext_ref, glu_ref, gp_ref, gn_ref, i == 0, i == pl.num_programs(1) - 1)

    def conv_rows(t, carry):
        r0 = pl.multiple_of(t * CONV_ROWS, CONV_ROWS)
        for c0 in range(0, BRANCH, LANES):
            win = ext_ref[pl.ds(r0, CONV_ROWS + 3 * CONV_HALO), c0:c0 + LANES]
            y_ref[pl.ds(r0, CONV_ROWS), c0:c0 + LANES] = _conv_piece(win, dw_ref, c0)
        return carry

    lax.fori_loop(0, tm // CONV_ROWS, conv_rows, 0)
    y = y_ref[...] + db_ref[...]
    c_lat = _silu(_layer_norm(y, cng_ref[...], cnb_ref[...]))
    c_mix = (c_lat * cg_ref[0].astype(F32)).astype(BF16)
    o_ref[0] = _out_proj_ln(c_mix, dm_ref[0], w_ref, h_ref[0], gate_ref[0], lg_ref[...], lb_ref[...], alpha)


def _cd_out(glu, cg, dm, h, mod, mod_row, dw_w, dw_b, cng, cnb, w_out, lg, lb, alpha, tm):
    bsz, n, _ = h.shape
    bspec = pl.BlockSpec((1, tm, BRANCH), lambda b, i: (b, i, 0))
    hspec = pl.BlockSpec((1, tm, D_MODEL), lambda b, i: (b, i, 0))
    vec = lambda width: pl.BlockSpec((1, width), lambda b, i: (0, 0))
    in_specs = _conv_specs(tm, n, BRANCH, 0) + [bspec, bspec, hspec] + _mod_specs(mod_row, (2,)) + [
        pl.BlockSpec(dw_w.shape, lambda b, i: (0, 0)),
        vec(BRANCH), vec(BRANCH), vec(BRANCH),
        pl.BlockSpec(w_out.shape, lambda b, i: (0, 0)),
        vec(D_MODEL), vec(D_MODEL),
    ]
    return pl.pallas_call(
        functools.partial(_cd_out_kernel, alpha=alpha),
        grid=(bsz, n // tm),
        in_specs=in_specs,
        out_specs=hspec,
        out_shape=jax.ShapeDtypeStruct((bsz, n, D_MODEL), F32),
        scratch_shapes=[pltpu.VMEM((tm + 3 * CONV_HALO, BRANCH), F32), pltpu.VMEM((tm, BRANCH), F32)],
        compiler_params=_params(2),
        name="cd_out",
    )(glu, glu, glu, cg, dm, h, mod, dw_w, dw_b, cng, cnb, w_out, lg, lb)


def kernel(x, c, ctx, c_ctx, mod_w, mod_b, ln_g, ln_b, ab_w_in, ab_w_out, a_w_s, a_b_s, a_norm_g, a_norm_b,
           b_lq1, b_lk1, b_lq2, b_lk2, b_subln_g, cd_w_in, cd_w_out, c_dw_w, c_dw_b, c_norm_g, c_norm_b, d_sink):
    bsz, n, _ = x.shape
    assert DEPTH == 2 and bsz + 1 <= MOD_ROWS
    alpha = (2.0 * DEPTH) ** 0.25
    tables = _rope_tables(n)

    cc = jnp.concatenate([c, c_ctx[None, :], jnp.zeros((MOD_ROWS - bsz - 1, D_MODEL), F32)], axis=0)
    mod = _modulation(cc, mod_w, mod_b).reshape(DEPTH * MOD_ROWS, 1, 3 * D_MODEL)
    lat_row = lambda layer: (lambda b: layer * MOD_ROWS + b)
    ctx_row = lambda layer: (lambda b: layer * MOD_ROWS + bsz)
    row2d = lambda a: a.reshape(1, -1)

    w_in0 = ab_w_in[0].astype(BF16)
    w_out0 = ab_w_out[0].astype(BF16)
    ws = a_w_s[0].astype(BF16)
    bs2d = jnp.repeat(a_b_s[0].T, LANES, axis=1)
    ng, nbias = row2d(a_norm_g[0]), row2d(a_norm_b[0])
    lvec = jnp.stack([b_lq1[0], b_lk1[0], b_lq2[0], b_lk2[0]])
    subln = row2d(b_subln_g[0])
    lam_init0 = 0.8 - 0.6 * math.exp(-0.3 * 0)
    lg0, lb0 = row2d(ln_g[0]), row2d(ln_b[0])

    ctx_len = ctx.shape[1]
    assert (bsz * ctx_len) % ROW_TILE == 0 and ROW_TILE % ctx_len == 0
    tiled = lambda a: a.reshape(-1, ROW_TILE, a.shape[-1])
    per_batch = lambda a: a.reshape(bsz, ctx_len, a.shape[-1])

    ta, vn, q, k, v, sg = _ab_in(x, mod, lat_row(0), w_in0, ng, nbias, tables, tm=ROW_TILE)
    cta, cvn, cq, ck, cv, csg = _ab_in(tiled(ctx), mod, ctx_row(0), w_in0, ng, nbias, None, tm=ROW_TILE)
    cta, cvn, cq, ck, cv, csg = map(per_batch, (cta, cvn, cq, ck, cv, csg))
    h1 = _ab_attn_out(lvec, subln, q, sg, ta, vn, x, mod, lat_row(0), ws, bs2d, w_out0, lg0, lb0,
                      [(ck, cv), (k, v)], lam_init0, alpha, tq=ATTN_TILE)
    hc1 = _ab_attn_out(lvec, subln, cq, csg, cta, cvn, ctx, mod, ctx_row(0), ws, bs2d, w_out0, lg0, lb0,
                       [(ck, cv)], lam_init0, alpha, tq=ctx_len)

    w_in1 = cd_w_in[0].astype(BF16)
    w_out1 = cd_w_out[0].astype(BF16)
    kv_lo = 4 * BRANCH
    glu, cg, dq, dk, dks, dv, dvs, dg = _cd_in(h1, mod, lat_row(1), w_in1, tables, tm=ROW_TILE)
    ck1, cks1, cv1, cvs1 = map(per_batch, _cd_ctx_kv(tiled(hc1), mod, ctx_row(1), w_in1[:, kv_lo:kv_lo + 2 * LANES]))
    dm = _swa(d_sink[0], dq, dg, dk, dks, dv, dvs, ck1, cks1, cv1, cvs1)
    return _cd_out(glu, cg, dm, h1, mod, lat_row(1), jnp.repeat(c_dw_w[0].astype(BF16), 16, axis=0),
                   row2d(c_dw_b[0]), row2d(c_norm_g[0]),
                   row2d(c_norm_b[0]), w_out1, row2d(ln_g[1]), row2d(ln_b[1]), alpha, tm=ROW_TILE)
```

```python
import functools
import math

import jax
import jax.numpy as jnp
import numpy as np
from jax import lax
from jax.experimental import pallas as pl
from jax.experimental.pallas import tpu as pltpu

F32 = jnp.float32
BF16 = jnp.bfloat16

D_MODEL = 1024
DEPTH = 2
CTX_LEN = 256
GRID_W = 64
ROPE_THETA = 10000.0
LN_EPS = 1e-6
RMS_EPS = 1e-5
NEG_INF = -1e30
HEAD_DIM = 64
BRANCH = 512
CHUNK = 128
C_KERNEL = 31
CONV_HALO = 16
MOD_ROWS = 24
LANES = 128

VMEM_LIMIT = 56 * 1024 * 1024
LOG2E = math.log2(math.e)
Q_SCALE = HEAD_DIM ** -0.5 * LOG2E
KEY_CHUNK = 256
CONV_ROWS = 128
ROW_TILE = 1024
ATTN_TILE = 512
ATTN_SUB = 256
SWA_BLOCKS = 8

_NT = (((1,), (1,)), ((), ()))


def _params(n_axes, vmem=VMEM_LIMIT):
    return pltpu.CompilerParams(dimension_semantics=("arbitrary",) * n_axes, vmem_limit_bytes=vmem)


def _layer_norm(x, g, b):
    mu = jnp.mean(x, axis=-1, keepdims=True)
    xc = x - mu
    var = jnp.mean(xc * xc, axis=-1, keepdims=True)
    return xc * lax.rsqrt(var + LN_EPS) * g + b


def _lane_mask(shape, lo):
    lane = lax.broadcasted_iota(jnp.int32, shape, len(shape) - 1) % LANES
    return (lane < HEAD_DIM) if lo else (lane >= HEAD_DIM)


def _mod_kernel(c_ref, w_ref, b_ref, o_ref):
    c = c_ref[...]
    a = c * jax.nn.sigmoid(c)
    a_hi = a.astype(BF16)
    a_lo = (a - a_hi.astype(F32)).astype(BF16)
    w = w_ref[0]
    w_hi = w.astype(BF16)
    w_lo = (w - w_hi.astype(F32)).astype(BF16)
    acc = jnp.dot(a_hi, w_hi, preferred_element_type=F32)
    acc = acc + jnp.dot(a_hi, w_lo, preferred_element_type=F32)
    acc = acc + jnp.dot(a_lo, w_hi, preferred_element_type=F32)
    o_ref[0] = acc + b_ref[0]


def _modulation(cc, mod_w, mod_b):
    tn = 768
    n_out = 3 * D_MODEL
    return pl.pallas_call(
        _mod_kernel,
        grid=(DEPTH, n_out // tn),
        in_specs=[
            pl.BlockSpec((MOD_ROWS, D_MODEL), lambda l, j: (0, 0)),
            pl.BlockSpec((1, D_MODEL, tn), lambda l, j: (l, 0, j)),
            pl.BlockSpec((1, 1, tn), lambda l, j: (l, 0, j)),
        ],
        out_specs=pl.BlockSpec((1, MOD_ROWS, tn), lambda l, j: (l, 0, j)),
        out_shape=jax.ShapeDtypeStruct((DEPTH, MOD_ROWS, n_out), F32),
        compiler_params=_params(2),
        name="modulation",
    )(cc, mod_w, mod_b.reshape(DEPTH, 1, n_out))


def _mod_specs(row_fn, parts):
    return [pl.BlockSpec((1, 1, D_MODEL), functools.partial(lambda b, i, p: (row_fn(b), 0, p), p=p))
            for p in parts]


def _rope_tables(n):
    m = HEAD_DIM // 4
    inv = ROPE_THETA ** (-np.arange(m, dtype=np.float64) / m)
    t = np.arange(n)
    ang_r = (t // GRID_W)[:, None] * inv
    ang_c = (t % GRID_W)[:, None] * inv
    z = np.zeros_like(ang_r)
    cos = np.concatenate([np.cos(ang_r)] * 2 + [np.cos(ang_c)] * 2, axis=-1)
    sin_up = np.concatenate([-np.sin(ang_r), z, -np.sin(ang_c), z], axis=-1)
    sin_dn = np.concatenate([z, np.sin(ang_r), z, np.sin(ang_c)], axis=-1)
    tile = lambda a: jnp.asarray(np.concatenate([a, a], axis=-1), dtype=F32)
    return tile(cos), tile(sin_up), tile(sin_dn)


def _rope(x, cos, sin_up, sin_dn):
    outs = []
    for s in range(x.shape[-1] // LANES):
        xs = x[:, s * LANES:(s + 1) * LANES]
        up = pltpu.roll(xs, LANES - 16, 1)
        dn = pltpu.roll(xs, 16, 1)
        outs.append(xs * cos + up * sin_up + dn * sin_dn)
    return outs[0] if len(outs) == 1 else jnp.concatenate(outs, axis=-1)


def _silu(x):
    return x * jax.nn.sigmoid(x)


def _ab_in_kernel(*refs, rope):
    if rope:
        (x_ref, shift_ref, scale_ref, w_ref, ng_ref, nb_ref, cos_ref, su_ref, sd_ref,
         ta_ref, vn_ref, q_ref, k_ref, v_ref, sg_ref) = refs
    else:
        (x_ref, shift_ref, scale_ref, w_ref, ng_ref, nb_ref,
         ta_ref, vn_ref, q_ref, k_ref, v_ref, sg_ref) = refs
    u = (x_ref[0] * (1.0 + scale_ref[0]) + shift_ref[0]).astype(BF16)

    def seg(s):
        return jnp.dot(u, w_ref[:, s * BRANCH:(s + 1) * BRANCH].astype(BF16), preferred_element_type=F32)

    ta_ref[0] = (jax.nn.gelu(seg(0)) * _silu(seg(2))).astype(BF16)
    vn_ref[0] = _layer_norm(jax.nn.gelu(seg(1)), ng_ref[...], nb_ref[...]).astype(BF16)
    q = seg(3)
    k = seg(4)
    if rope:
        tabs = (cos_ref[...], su_ref[...], sd_ref[...])
        q = _rope(q, *tabs)
        k = _rope(k, *tabs)
    q_ref[0] = (q * Q_SCALE).astype(BF16)
    k_ref[0] = k.astype(BF16)
    v_ref[0] = seg(5).astype(BF16)
    sg_ref[0] = _silu(seg(6)).astype(BF16)


def _ab_in(h, mod, mod_row, w_in, ng, nb, tables, tm):
    bsz, n, _ = h.shape
    rope = tables is not None
    in_specs = [pl.BlockSpec((1, tm, D_MODEL), lambda b, i: (b, i, 0))]
    in_specs += _mod_specs(mod_row, (0, 1))
    in_specs += [
        pl.BlockSpec(w_in.shape, lambda b, i: (0, 0), pipeline_mode=pl.Buffered(1)),
        pl.BlockSpec((1, BRANCH), lambda b, i: (0, 0)),
        pl.BlockSpec((1, BRANCH), lambda b, i: (0, 0)),
    ]
    args = [h, mod, mod, w_in, ng, nb]
    if rope:
        in_specs += [pl.BlockSpec((tm, LANES), lambda b, i: (i, 0))] * 3
        args += list(tables)
    out_spec = pl.BlockSpec((1, tm, BRANCH), lambda b, i: (b, i, 0))
    out_shape = jax.ShapeDtypeStruct((bsz, n, BRANCH), BF16)
    return pl.pallas_call(
        functools.partial(_ab_in_kernel, rope=rope),
        grid=(bsz, n // tm),
        in_specs=in_specs,
        out_specs=[out_spec] * 6,
        out_shape=[out_shape] * 6,
        compiler_params=_params(2),
        name="ab_in_rope" if rope else "ab_in_ctx",
    )(*args)


def _out_proj_ln(left, right, w_ref, h, gate, lg, lb, alpha):
    y = jnp.dot(left, w_ref[:BRANCH, :].astype(BF16), preferred_element_type=F32)
    y = y + jnp.dot(right, w_ref[BRANCH:, :].astype(BF16), preferred_element_type=F32)
    return _layer_norm(alpha * h + gate * y, lg, lb)


def _ab_attn_out_kernel(*refs, seg_rows, lam_init, alpha):
    n_seg = len(seg_rows)
    (lv_ref, g_ref, q_ref, sg_ref, ta_ref, vn_ref, h_ref, gate_ref, ws_ref, bs_ref, w_ref, lg_ref,
     lb_ref) = refs[:13]
    kv = refs[13:13 + 2 * n_seg]
    o_ref = refs[13 + 2 * n_seg]
    vext_ref = refs[14 + 2 * n_seg]
    bm_ref = refs[15 + 2 * n_seg]
    heads = BRANCH // LANES

    @pl.when(pl.program_id(1) == 0)
    def _fill_values():
        for h in range(heads):
            base = 0
            for j, rows in enumerate(seg_rows):
                vext_ref[h, base:base + rows, :LANES] = kv[2 * j + 1][0, :, h * LANES:(h + 1) * LANES]
                base += rows
            vext_ref[h, :, LANES:] = jnp.ones((base, LANES), BF16)

    lv = lv_ref[...]
    lam = (jnp.exp(jnp.sum(lv[0:1] * lv[1:2], axis=-1, keepdims=True))
           - jnp.exp(jnp.sum(lv[2:3] * lv[3:4], axis=-1, keepdims=True)) + lam_init)
    tq = min(q_ref.shape[1], ATTN_SUB)
    lo = _lane_mask((tq, LANES), True)

    def attend(r0, h):
        cols = slice(h * LANES, (h + 1) * LANES)
        q = q_ref[0, r0:r0 + tq, cols]
        zero = jnp.zeros_like(q)
        qs = jnp.concatenate([jnp.where(lo, q, zero), jnp.where(lo, zero, q)], axis=0)
        m = acc = None
        base = 0
        for j, rows in enumerate(seg_rows):
            for c0 in range(0, rows, KEY_CHUNK):
                n = min(KEY_CHUNK, rows - c0)
                s = lax.dot_general(qs, kv[2 * j][0, c0:c0 + n, cols], _NT, preferred_element_type=F32)
                smax = functools.reduce(jnp.maximum, [s[:, l0:l0 + LANES] for l0 in range(0, n, LANES)])
                smax = jnp.max(smax, axis=-1, keepdims=True)
                m_new = smax if m is None else jnp.maximum(m, smax)
                p = jnp.exp2(s - m_new).astype(BF16)
                pv = jnp.dot(p, vext_ref[h, base + c0:base + c0 + n, :], preferred_element_type=F32)
                acc = pv if acc is None else jnp.exp2(m - m_new) * acc + pv
                m = m_new
            base += rows
        on = acc[:, :LANES] * (1.0 / acc[:, LANES:])
        o = on[:tq] - lam * on[tq:]
        o = o * lax.rsqrt(jnp.mean(o * o, axis=-1, keepdims=True) + RMS_EPS) * g_ref[...] * (1.0 - lam_init)
        bm_ref[r0:r0 + tq, cols] = (o * sg_ref[0, r0:r0 + tq, cols].astype(F32)).astype(BF16)

    def mix_and_project(r0):
        rows = []
        for c0 in range(r0, r0 + tq, CHUNK):
            cols = [jnp.dot(ws_ref[g].astype(BF16), vn_ref[0, c0:c0 + CHUNK, g * LANES:(g + 1) * LANES],
                            preferred_element_type=F32) for g in range(heads)]
            rows.append(jnp.concatenate(cols, axis=-1) + bs_ref[...])
        mixed = rows[0] if len(rows) == 1 else jnp.concatenate(rows, axis=0)
        a_mix = (ta_ref[0, r0:r0 + tq, :].astype(F32) * mixed).astype(BF16)
        o_ref[0, r0:r0 + tq, :] = _out_proj_ln(a_mix, bm_ref[r0:r0 + tq, :], w_ref, h_ref[0, r0:r0 + tq, :],
                                               gate_ref[0], lg_ref[...], lb_ref[...], alpha)

    starts = list(range(0, q_ref.shape[1], tq))
    for idx, r0 in enumerate(starts):
        for h in range(heads):
            attend(r0, h)
            if h == 0 and idx > 0:
                mix_and_project(starts[idx - 1])
    mix_and_project(starts[-1])


def _ab_attn_out(lvec, subln_g, q, sg, ta, vn, h, mod, mod_row, ws, bs2d, w_out, lg, lb, segs, lam_init, alpha, tq):
    bsz, n, _ = q.shape
    qspec = pl.BlockSpec((1, tq, BRANCH), lambda b, i: (b, i, 0))
    hspec = pl.BlockSpec((1, tq, D_MODEL), lambda b, i: (b, i, 0))
    whole = lambda a: pl.BlockSpec(a.shape, lambda b, i: (0,) * a.ndim)
    in_specs = [whole(lvec), whole(subln_g), qspec, qspec, qspec, qspec, hspec] + _mod_specs(mod_row, (2,)) + [
        whole(ws), whole(bs2d), pl.BlockSpec(w_out.shape, lambda b, i: (0, 0), pipeline_mode=pl.Buffered(1)),
        whole(lg), whole(lb)]
    args = [lvec, subln_g, q, sg, ta, vn, h, mod, ws, bs2d, w_out, lg, lb]
    for k, v in segs:
        spec = pl.BlockSpec((1, k.shape[1], BRANCH), lambda b, i: (b, 0, 0))
        in_specs += [spec, spec]
        args += [k, v]
    seg_rows = tuple(k.shape[1] for k, _ in segs)
    return pl.pallas_call(
        functools.partial(_ab_attn_out_kernel, seg_rows=seg_rows, lam_init=lam_init, alpha=alpha),
        grid=(bsz, n // tq),
        in_specs=in_specs,
        out_specs=hspec,
        out_shape=jax.ShapeDtypeStruct((bsz, n, D_MODEL), F32),
        scratch_shapes=[pltpu.VMEM((BRANCH // LANES, sum(seg_rows), 2 * LANES), BF16),
                        pltpu.VMEM((tq, BRANCH), BF16)],
        compiler_params=_params(2),
        name=f"ab_attn_out_{len(segs)}seg",
    )(*args)


def _cd_in_kernel(x_ref, shift_ref, scale_ref, w_ref, cos_ref, su_ref, sd_ref,
                  glu_ref, cg_ref, q_ref, k_ref, ks_ref, v_ref, vs_ref, dg_ref):
    u = (x_ref[0] * (1.0 + scale_ref[0]) + shift_ref[0]).astype(BF16)

    def cols(lo, width):
        return jnp.dot(u, w_ref[:, lo:lo + width].astype(BF16), preferred_element_type=F32)

    glu_ref[0] = cols(0, BRANCH) * jax.nn.sigmoid(cols(BRANCH, BRANCH))
    cg_ref[0] = _silu(cols(2 * BRANCH, BRANCH)).astype(BF16)
    tabs = (cos_ref[...], su_ref[...], sd_ref[...])
    q_ref[0] = (_rope(cols(3 * BRANCH, BRANCH), *tabs) * Q_SCALE).astype(BF16)
    k = _rope(cols(4 * BRANCH, LANES), *tabs)
    v = cols(4 * BRANCH + LANES, LANES)
    k_ref[0] = k.astype(BF16)
    ks_ref[0] = pltpu.roll(k, HEAD_DIM, 1).astype(BF16)
    v_ref[0] = v.astype(BF16)
    vs_ref[0] = pltpu.roll(v, HEAD_DIM, 1).astype(BF16)
    dg_ref[0] = _silu(cols(4 * BRANCH + 2 * LANES, BRANCH)).astype(BF16)


def _cd_in(h, mod, mod_row, w_in, tables, tm):
    bsz, n, _ = h.shape
    in_specs = [pl.BlockSpec((1, tm, D_MODEL), lambda b, i: (b, i, 0))] + _mod_specs(mod_row, (0, 1))
    in_specs += [pl.BlockSpec(w_in.shape, lambda b, i: (0, 0), pipeline_mode=pl.Buffered(1))]
    in_specs += [pl.BlockSpec((tm, LANES), lambda b, i: (i, 0))] * 3
    wide = pl.BlockSpec((1, tm, BRANCH), lambda b, i: (b, i, 0))
    narrow = pl.BlockSpec((1, tm, LANES), lambda b, i: (b, i, 0))
    wide_bf = jax.ShapeDtypeStruct((bsz, n, BRANCH), BF16)
    narrow_bf = jax.ShapeDtypeStruct((bsz, n, LANES), BF16)
    return pl.pallas_call(
        _cd_in_kernel,
        grid=(bsz, n // tm),
        in_specs=in_specs,
        out_specs=[wide, wide, wide, narrow, narrow, narrow, narrow, wide],
        out_shape=[jax.ShapeDtypeStruct((bsz, n, BRANCH), F32), wide_bf, wide_bf,
                   narrow_bf, narrow_bf, narrow_bf, narrow_bf, wide_bf],
        compiler_params=_params(2),
        name="cd_in",
    )(h, mod, mod, w_in, *tables)


def _cd_ctx_kv_kernel(x_ref, shift_ref, scale_ref, w_ref, k_ref, ks_ref, v_ref, vs_ref):
    u = (x_ref[0] * (1.0 + scale_ref[0]) + shift_ref[0]).astype(BF16)
    kv = jnp.dot(u, w_ref[...].astype(BF16), preferred_element_type=F32)
    k = kv[:, :LANES]
    v = kv[:, LANES:]
    k_ref[0] = k.astype(BF16)
    ks_ref[0] = pltpu.roll(k, HEAD_DIM, 1).astype(BF16)
    v_ref[0] = v.astype(BF16)
    vs_ref[0] = pltpu.roll(v, HEAD_DIM, 1).astype(BF16)


def _cd_ctx_kv(h, mod, mod_row, w_kv):
    bsz, n, _ = h.shape
    in_specs = [pl.BlockSpec((1, n, D_MODEL), lambda b: (b, 0, 0))]
    in_specs += [pl.BlockSpec((1, 1, D_MODEL), functools.partial(lambda b, p: (mod_row(b), 0, p), p=p))
                 for p in (0, 1)]
    in_specs += [pl.BlockSpec(w_kv.shape, lambda b: (0, 0))]
    spec = pl.BlockSpec((1, n, LANES), lambda b: (b, 0, 0))
    shape = jax.ShapeDtypeStruct((bsz, n, LANES), BF16)
    return pl.pallas_call(
        _cd_ctx_kv_kernel,
        grid=(bsz,),
        in_specs=in_specs,
        out_specs=[spec] * 4,
        out_shape=[shape] * 4,
        compiler_params=_params(1),
        name="cd_ctx_kv",
    )(h, mod, mod, w_kv)


def _conv_piece(win, dw_ref, c0):
    rows = win.shape[0] - 3 * CONV_HALO
    off = CONV_HALO - C_KERNEL // 2
    span = rows + 16
    even = win[0:rows + 32].astype(BF16)
    odd = win[8:rows + 40].astype(BF16)
    y = None
    for s in range(8):
        z = None
        for a in range((off + C_KERNEL + 7) // 8):
            j = 8 * a + s - off
            if 0 <= j < C_KERNEL:
                src = odd if a % 2 else even
                base = 16 * (a // 2)
                w = pltpu.repeat(dw_ref[16 * j:16 * j + 16, c0:c0 + LANES], span // 16, axis=0)
                term = src[base:base + span] * w
                z = term if z is None else z + term
        zs = z.astype(F32)[s:s + rows]
        y = zs if y is None else y + zs
    return y


def _fill_conv_window(ext_ref, glu_ref, gp_ref, gn_ref, first, last):
    tm = glu_ref.shape[1]
    ext_ref[0:CONV_HALO, :] = jnp.where(first, 0.0, gp_ref[0])
    ext_ref[CONV_HALO:CONV_HALO + tm, :] = glu_ref[0]
    ext_ref[CONV_HALO + tm:2 * CONV_HALO + tm, :] = jnp.where(last, 0.0, gn_ref[0])
    ext_ref[2 * CONV_HALO + tm:, :] = jnp.zeros((CONV_HALO, ext_ref.shape[1]), F32)


def _conv_specs(tm, n, width, lane_block):
    per = tm // CONV_HALO
    last = n // CONV_HALO - 1
    return [
        pl.BlockSpec((1, tm, width), lambda b, i: (b, i, lane_block)),
        pl.BlockSpec((1, CONV_HALO, width), lambda b, i: (b, jnp.maximum(i * per - 1, 0), lane_block)),
        pl.BlockSpec((1, CONV_HALO, width), lambda b, i: (b, jnp.minimum((i + 1) * per, last), lane_block)),
    ]


def _swa_kernel(sink_ref, q_ref, dg_ref,
                kp_ref, kc_ref, kn_ref, ksp_ref, ksc_ref, ksn_ref,
                vp_ref, vc_ref, vn_ref, vsp_ref, vsc_ref, vsn_ref,
                ck_ref, cks_ref, cv_ref, cvs_ref, o_ref):
    i = pl.program_id(1)
    nsteps = pl.num_programs(1)
    n_slab = BRANCH // LANES
    stack = lambda a: jnp.concatenate([a] * n_slab, axis=0)
    r = lax.broadcasted_iota(jnp.int32, (CHUNK, CHUNK), 0)
    c = lax.broadcasted_iota(jnp.int32, (CHUNK, CHUNK), 1)
    tri_prev = stack(c >= r)
    tri_next = stack(c <= r)
    edge_prev = stack(c >= r + jnp.where(i > 0, 0, CHUNK))
    edge_next = stack(c <= r - jnp.where(i < nsteps - 1, 0, CHUNK))
    lo = _lane_mask((CHUNK, LANES), True)
    row = lax.broadcasted_iota(jnp.int32, (n_slab * CHUNK, 1), 0)
    blk = lambda t: slice(t * CHUNK, (t + 1) * CHUNK)

    layouts = (
        ((0, 2, 5, 7), (kp_ref, kc_ref, kn_ref, ck_ref), (vp_ref, vc_ref, vn_ref, cv_ref)),
        ((1, 3, 4, 6), (ksp_ref, ksc_ref, ksn_ref, cks_ref), (vsp_ref, vsc_ref, vsn_ref, cvs_ref)),
    )
    split = lambda a: [a[:, c0:c0 + LANES] for c0 in range(0, a.shape[1], LANES)]

    def band(t, refs):
        prev, cur, nxt, ctx = refs
        before = prev[0] if t == 0 else cur[0, blk(t - 1)]
        after = nxt[0] if t == SWA_BLOCKS - 1 else cur[0, blk(t + 1)]
        return before, cur[0, blk(t)], after, ctx[0]

    def scores(t, swapped):
        slabs = [q_ref[0, blk(t), j * LANES:(j + 1) * LANES] for j in range(n_slab)]
        zero = jnp.zeros_like(slabs[0])
        q_lo = [jnp.where(lo, q, zero) for q in slabs]
        q_hi = [jnp.where(lo, zero, q) for q in slabs]
        qs = jnp.concatenate((q_hi[:2] + q_lo[2:]) if swapped else (q_lo[:2] + q_hi[2:]), axis=0)
        masks = (edge_prev if t == 0 else tri_prev, None, edge_next if t == SWA_BLOCKS - 1 else tri_next, None)
        ss = [lax.dot_general(qs, kb, _NT, preferred_element_type=F32) for kb in band(t, layouts[swapped][1])]
        return [s if mk is None else jnp.where(mk, s, NEG_INF) for s, mk in zip(ss, masks)]

    def softmax(ss, swapped):
        sink = jnp.zeros((n_slab * CHUNK, 1), F32)
        for j, head in enumerate(layouts[swapped][0]):
            sink = jnp.where((row >= j * CHUNK) & (row < (j + 1) * CHUNK), sink_ref[head] * LOG2E, sink)
        m = jnp.max(functools.reduce(jnp.maximum, sum((split(s) for s in ss), [])), axis=-1, keepdims=True)
        m = jnp.maximum(m, sink)
        ps = [jnp.exp2(s - m) for s in ss]
        l = jnp.sum(functools.reduce(jnp.add, sum((split(p) for p in ps), [])), axis=-1, keepdims=True)
        l = l + jnp.exp2(sink - m)
        return [p.astype(BF16) for p in ps], 1.0 / l

    def weighted_values(ps, inv_l, t, swapped):
        acc = None
        for p, vb in zip(ps, band(t, layouts[swapped][2])):
            pv = jnp.dot(p, vb, preferred_element_type=F32)
            acc = pv if acc is None else acc + pv
        return acc * inv_l

    def emit(t, plain, swapped):
        out = []
        for j in range(n_slab):
            a = plain[j * CHUNK:(j + 1) * CHUNK]
            b = swapped[j * CHUNK:(j + 1) * CHUNK]
            out.append(jnp.where(lo, a, b) if j < 2 else jnp.where(lo, b, a))
        o = jnp.concatenate(out, axis=-1)
        o_ref[0, blk(t), :] = (o * dg_ref[0, blk(t), :].astype(F32)).astype(BF16)

    groups = [(t, swapped) for t in range(SWA_BLOCKS) for swapped in (0, 1)]
    ss_next = scores(*groups[0])
    normed = []
    for idx, (t, swapped) in enumerate(groups):
        ss = ss_next
        if idx + 1 < len(groups):
            ss_next = scores(*groups[idx + 1])
        ps, inv_l = softmax(ss, swapped)
        normed.append(weighted_values(ps, inv_l, t, swapped))
        if swapped:
            emit(t, normed[-2], normed[-1])


def _swa(sink, q, dg, k, ks, v, vs, ck, cks, cv, cvs):
    bsz, n, _ = q.shape
    nb = n // CHUNK
    tq = SWA_BLOCKS * CHUNK
    qspec = pl.BlockSpec((1, tq, BRANCH), lambda b, i: (b, i, 0))
    prev = pl.BlockSpec((1, CHUNK, LANES), lambda b, i: (b, jnp.maximum(SWA_BLOCKS * i - 1, 0), 0))
    cur = pl.BlockSpec((1, tq, LANES), lambda b, i: (b, i, 0))
    nxt = pl.BlockSpec((1, CHUNK, LANES), lambda b, i: (b, jnp.minimum(SWA_BLOCKS * (i + 1), nb - 1), 0))
    cspec = pl.BlockSpec((1, ck.shape[1], LANES), lambda b, i: (b, 0, 0))
    in_specs = [pl.BlockSpec(memory_space=pltpu.SMEM), qspec, qspec] + [prev, cur, nxt] * 4 + [cspec] * 4
    return pl.pallas_call(
        _swa_kernel,
        grid=(bsz, nb // SWA_BLOCKS),
        in_specs=in_specs,
        out_specs=qspec,
        out_shape=jax.ShapeDtypeStruct((bsz, n, BRANCH), BF16),
        compiler_params=_params(2),
        name="swa",
    )(sink, q, dg, k, k, k, ks, ks, ks, v, v, v, vs, vs, vs, ck, cks, cv, cvs)


def _cd_out_kernel(glu_ref, gp_ref, gn_ref, cg_ref, dm_ref, h_ref, gate_ref, dw_ref, db_ref, cng_ref, cnb_ref,
                   w_ref, lg_ref, lb_ref, o_ref, ext_ref, y_ref, *, alpha):
    i = pl.program_id(1)
    tm = glu_ref.shape[1]
    _fill_conv_window(ext_ref, glu_ref, gp_ref, gn_ref, i == 0, i == pl.num_programs(1) - 1)

    def conv_rows(t, carry):
        r0 = pl.multiple_of(t * CONV_ROWS, CONV_ROWS)
        for c0 in range(0, BRANCH, LANES):
            win = ext_ref[pl.ds(r0, CONV_ROWS + 3 * CONV_HALO), c0:c0 + LANES]
            y_ref[pl.ds(r0, CONV_ROWS), c0:c0 + LANES] = _conv_piece(win, dw_ref, c0)
        return carry

    lax.fori_loop(0, tm // CONV_ROWS, conv_rows, 0)
    y = y_ref[...] + db_ref[...]
    c_lat = _silu(_layer_norm(y, cng_ref[...], cnb_ref[...]))
    c_mix = (c_lat * cg_ref[0].astype(F32)).astype(BF16)
    o_ref[0] = _out_proj_ln(c_mix, dm_ref[0], w_ref, h_ref[0], gate_ref[0], lg_ref[...], lb_ref[...], alpha)


def _cd_out(glu, cg, dm, h, mod, mod_row, dw_w, dw_b, cng, cnb, w_out, lg, lb, alpha, tm):
    bsz, n, _ = h.shape
    bspec = pl.BlockSpec((1, tm, BRANCH), lambda b, i: (b, i, 0))
    hspec = pl.BlockSpec((1, tm, D_MODEL), lambda b, i: (b, i, 0))
    vec = lambda width: pl.BlockSpec((1, width), lambda b, i: (0, 0))
    in_specs = _conv_specs(tm, n, BRANCH, 0) + [bspec, bspec, hspec] + _mod_specs(mod_row, (2,)) + [
        pl.BlockSpec(dw_w.shape, lambda b, i: (0, 0)),
        vec(BRANCH), vec(BRANCH), vec(BRANCH),
        pl.BlockSpec(w_out.shape, lambda b, i: (0, 0), pipeline_mode=pl.Buffered(1)),
        vec(D_MODEL), vec(D_MODEL),
    ]
    return pl.pallas_call(
        functools.partial(_cd_out_kernel, alpha=alpha),
        grid=(bsz, n // tm),
        in_specs=in_specs,
        out_specs=hspec,
        out_shape=jax.ShapeDtypeStruct((bsz, n, D_MODEL), F32),
        scratch_shapes=[pltpu.VMEM((tm + 3 * CONV_HALO, BRANCH), F32), pltpu.VMEM((tm, BRANCH), F32)],
        compiler_params=_params(2),
        name="cd_out",
    )(glu, glu, glu, cg, dm, h, mod, dw_w, dw_b, cng, cnb, w_out, lg, lb)


def kernel(x, c, ctx, c_ctx, mod_w, mod_b, ln_g, ln_b, ab_w_in, ab_w_out, a_w_s, a_b_s, a_norm_g, a_norm_b,
           b_lq1, b_lk1, b_lq2, b_lk2, b_subln_g, cd_w_in, cd_w_out, c_dw_w, c_dw_b, c_norm_g, c_norm_b, d_sink):
    bsz, n, _ = x.shape
    assert DEPTH == 2 and bsz + 1 <= MOD_ROWS
    alpha = (2.0 * DEPTH) ** 0.25
    tables = _rope_tables(n)

    cc = jnp.concatenate([c, c_ctx[None, :], jnp.zeros((MOD_ROWS - bsz - 1, D_MODEL), F32)], axis=0)
    mod = _modulation(cc, mod_w, mod_b).reshape(DEPTH * MOD_ROWS, 1, 3 * D_MODEL)
    lat_row = lambda layer: (lambda b: layer * MOD_ROWS + b)
    ctx_row = lambda layer: (lambda b: layer * MOD_ROWS + bsz)
    row2d = lambda a: a.reshape(1, -1)

    w_in0, w_out0, ws = ab_w_in[0], ab_w_out[0], a_w_s[0]
    bs2d = jnp.repeat(a_b_s[0].T, LANES, axis=1)
    ng, nbias = row2d(a_norm_g[0]), row2d(a_norm_b[0])
    lvec = jnp.stack([b_lq1[0], b_lk1[0], b_lq2[0], b_lk2[0]])
    subln = row2d(b_subln_g[0])
    lam_init0 = 0.8 - 0.6 * math.exp(-0.3 * 0)
    lg0, lb0 = row2d(ln_g[0]), row2d(ln_b[0])

    ctx_len = ctx.shape[1]
    assert (bsz * ctx_len) % ROW_TILE == 0 and ROW_TILE % ctx_len == 0
    tiled = lambda a: a.reshape(-1, ROW_TILE, a.shape[-1])
    per_batch = lambda a: a.reshape(bsz, ctx_len, a.shape[-1])

    ta, vn, q, k, v, sg = _ab_in(x, mod, lat_row(0), w_in0, ng, nbias, tables, tm=ROW_TILE)
    cta, cvn, cq, ck, cv, csg = _ab_in(tiled(ctx), mod, ctx_row(0), w_in0, ng, nbias, None, tm=ROW_TILE)
    cta, cvn, cq, ck, cv, csg = map(per_batch, (cta, cvn, cq, ck, cv, csg))
    h1 = _ab_attn_out(lvec, subln, q, sg, ta, vn, x, mod, lat_row(0), ws, bs2d, w_out0, lg0, lb0,
                      [(ck, cv), (k, v)], lam_init0, alpha, tq=ATTN_TILE)
    hc1 = _ab_attn_out(lvec, subln, cq, csg, cta, cvn, ctx, mod, ctx_row(0), ws, bs2d, w_out0, lg0, lb0,
                       [(ck, cv)], lam_init0, alpha, tq=ctx_len)

    w_in1, w_out1 = cd_w_in[0], cd_w_out[0]
    kv_lo = 4 * BRANCH
    glu, cg, dq, dk, dks, dv, dvs, dg = _cd_in(h1, mod, lat_row(1), w_in1, tables, tm=ROW_TILE)
    ck1, cks1, cv1, cvs1 = map(per_batch, _cd_ctx_kv(tiled(hc1), mod, ctx_row(1), w_in1[:, kv_lo:kv_lo + 2 * LANES]))
    dm = _swa(d_sink[0], dq, dg, dk, dks, dv, dvs, ck1, cks1, cv1, cvs1)
    return _cd_out(glu, cg, dm, h1, mod, lat_row(1), jnp.repeat(c_dw_w[0].astype(BF16), 16, axis=0),
                   row2d(c_dw_b[0]), row2d(c_norm_g[0]),
                   row2d(c_norm_b[0]), w_out1, row2d(ln_g[1]), row2d(ln_b[1]), alpha, tm=ROW_TILE)
```
